```python
import jax, jax.numpy as jnp
from jax import lax
import numpy as np

D_MODEL = 1024
BATCH = 8
SEQ = 2048
DEPTH = 4
DEC_BATCH = 128
DEC_SEQ = 8
PAST_LEN = 8192
PAGE_SIZE = 128

N_MIXERS = 3
N_SB = (DEPTH + 2) // 3
N_ML = (DEPTH + 1) // 3
N_MLA = DEPTH // 3
BLOCK_Q = 128
EPS = 1e-6

SB_HEADS = 16
SB_KV_HEADS = 4
SB_GROUP = SB_HEADS // SB_KV_HEADS
SB_HEAD_DIM = D_MODEL // SB_HEADS
SB_SCALE = SB_HEAD_DIM ** -0.5

ML_HEADS = 4
ML_DV = D_MODEL // ML_HEADS
ML_DK = ML_DV // 2
ML_QK = ML_HEADS * ML_DK
ML_V = ML_HEADS * ML_DV
ML_IN = 2 * ML_QK + ML_V + D_MODEL + 2 * ML_HEADS
ML_CHUNK = 128
ML_FORGET_BIAS = 3.0

MLA_HEADS = 8
MLA_NOPE = 128
MLA_ROPE = 64
MLA_VDIM = 128
MLA_Q_LORA = 512
MLA_KV_LORA = 512
MLA_SCALE = (MLA_NOPE + MLA_ROPE) ** -0.5
ROPE_THETA = 10000.0

N_MEM = 256
XA_HEADS = 4
XA_HEAD_DIM = D_MODEL // XA_HEADS
XA_SCALE = XA_HEAD_DIM ** -0.5

D_FF = 2816
CONV_W = 3
N_NORMS = 7

kernel_name = 'hybrid_sb_mlstm_mla_decoder_step'


def rmsnorm(x, g):
    xf = x.astype(jnp.float32)
    y = xf * lax.rsqrt(jnp.mean(xf * xf, axis=-1, keepdims=True) + EPS)
    return (y * g.astype(jnp.float32)).astype(x.dtype)


def rope(x, pos):
    half = x.shape[-1] // 2
    inv = ROPE_THETA ** (-jnp.arange(half, dtype=jnp.float32) / half)
    ang = pos.astype(jnp.float32)[:, None] * inv[None, :]
    shape = (ang.shape[0],) + (1,) * (x.ndim - 3) + (half,)
    cos, sin = jnp.cos(ang).reshape(shape), jnp.sin(ang).reshape(shape)
    xf = x.astype(jnp.float32)
    x1, x2 = xf[..., :half], xf[..., half:]
    return jnp.concatenate([x1 * cos - x2 * sin, x2 * cos + x1 * sin], axis=-1).astype(x.dtype)


def gather_pages(pool, layer, page_table):
    g = pool[layer, page_table]
    return g.reshape((g.shape[0], g.shape[1] * g.shape[2]) + g.shape[3:])


def blocked(attend, q_parts, past_segs, new_keys, q_pos):
    n_q = q_parts[0].shape[1]
    outs = []
    for start in range(0, n_q, BLOCK_Q):
        end = min(start + BLOCK_Q, n_q)
        segs = past_segs + [tuple(a[:, :end] for a in new_keys) + (q_pos[:end],)]
        outs.append(attend(tuple(p[:, start:end] for p in q_parts), segs, q_pos[start:end]))
    return jnp.concatenate(outs, axis=1)


def sb_attend(q_parts, segs, q_pos):
    (q,) = q_parts
    z = jnp.concatenate([jnp.einsum('bqhgd,bkhd->bhgqk', q, k.astype(q.dtype), preferred_element_type=jnp.float32)
                         for k, _, _ in segs], axis=-1) * SB_SCALE
    k_pos = jnp.concatenate([p for _, _, p in segs])
    valid = k_pos[None, :] < q_pos[:, None]
    log_stay = jnp.where(valid, jax.nn.log_sigmoid(-z), 0.0)
    between = lax.cumsum(log_stay, axis=4, reverse=True) - log_stay
    a = jnp.where(valid, jnp.exp(jax.nn.log_sigmoid(z) + between), 0.0).astype(q.dtype)
    out, off = None, 0
    for _, v, p in segs:
        n = p.shape[0]
        part = jnp.einsum('bhgqk,bkhd->bqhgd', a[..., off:off + n], v.astype(q.dtype))
        out = part if out is None else out + part
        off += n
    return out


def mla_attend(q_parts, segs, q_pos):
    q_lat, q_rope = q_parts
    s = jnp.concatenate([jnp.einsum('bqhc,bkc->bhqk', q_lat, c.astype(q_lat.dtype), preferred_element_type=jnp.float32)
                         + jnp.einsum('bqhr,bkr->bhqk', q_rope, r.astype(q_lat.dtype), preferred_element_type=jnp.float32)
                         for c, r, _ in segs], axis=-1) * MLA_SCALE
    k_pos = jnp.concatenate([p for _, _, p in segs])
    s = jnp.where(k_pos[None, :] <= q_pos[:, None], s, -jnp.inf)
    pr = jax.nn.softmax(s, axis=-1).astype(q_lat.dtype)
    out, off = None, 0
    for c, _, p in segs:
        n = p.shape[0]
        part = jnp.einsum('bhqk,bkc->bqhc', pr[..., off:off + n], c.astype(q_lat.dtype))
        out = part if out is None else out + part
        off += n
    return out


def sb_mixer(h, q_pos, past_segs, w_qkv, w_o):
    B, S, _ = h.shape
    nq, nkv = SB_HEADS * SB_HEAD_DIM, SB_KV_HEADS * SB_HEAD_DIM
    qkv = h @ w_qkv
    q = qkv[..., :nq].reshape(B, S, SB_KV_HEADS, SB_GROUP, SB_HEAD_DIM)
    k = qkv[..., nq:nq + nkv].reshape(B, S, SB_KV_HEADS, SB_HEAD_DIM)
    v = qkv[..., nq + nkv:].reshape(B, S, SB_KV_HEADS, SB_HEAD_DIM)
    o = blocked(sb_attend, (q,), past_segs, (k, v), q_pos)
    return o.reshape(B, S, nq) @ w_o, k, v


def mla_mixer(h, pos, past_segs, w_dq, g_q, w_uq, w_dkv, g_kv, w_uk, w_uv, w_o):
    B, S, _ = h.shape
    cq = rmsnorm(h @ w_dq, g_q)
    qf = (cq @ w_uq).reshape(B, S, MLA_HEADS, MLA_NOPE + MLA_ROPE)
    q_lat = jnp.einsum('bshn,chn->bshc', qf[..., :MLA_NOPE], w_uk)
    q_rope = rope(qf[..., MLA_NOPE:], pos)
    kv = h @ w_dkv
    ckv = rmsnorm(kv[..., :MLA_KV_LORA], g_kv)
    kr = rope(kv[..., MLA_KV_LORA:], pos)
    o_lat = blocked(mla_attend, (q_lat, q_rope), past_segs, (ckv, kr), pos)
    o = jnp.einsum('bshc,chv->bshv', o_lat, w_uv).reshape(B, S, MLA_HEADS * MLA_VDIM)
    return o @ w_o, ckv, kr


def mlstm_chunkwise(q, k, v, i_pre, log_f, C0, n0, m0, chunk):
    f32 = jnp.float32
    B, S, H, _ = q.shape
    nc = S // chunk

    def to_chunks(t):
        return jnp.moveaxis(t.astype(f32).reshape((B, nc, chunk) + t.shape[2:]), 1, 0)

    xs = (to_chunks(q), to_chunks(k), to_chunks(v), to_chunks(i_pre), to_chunks(log_f))
    causal = jnp.tril(jnp.ones((chunk, chunk), dtype=bool))

    def step(carry, inp):
        C, n, m = carry
        qc, kc, vc, ic, lfc = inp
        b = jnp.moveaxis(jnp.cumsum(lfc, axis=1), 1, 2)
        it = jnp.moveaxis(ic, 1, 2)
        d = jnp.where(causal, b[..., :, None] - b[..., None, :] + it[..., None, :], -jnp.inf)
        inter = b + m[..., None]
        m_q = jnp.maximum(inter, jnp.max(d, axis=-1))
        sc = jnp.einsum('blhd,bshd->bhls', qc, kc) * jnp.exp(d - m_q[..., None])
        w_st = jnp.exp(inter - m_q)
        num = jnp.einsum('bhls,bshe->bhle', sc, vc) + w_st[..., None] * jnp.einsum('bhed,blhd->bhle', C, qc)
        den = jnp.sum(sc, axis=-1) + w_st * jnp.einsum('bhd,blhd->bhl', n, qc)
        hc = num / jnp.maximum(jnp.abs(den), jnp.exp(-m_q))[..., None]
        b_end = b[..., -1]
        g = b_end[..., None] - b + it
        m_new = jnp.maximum(b_end + m, jnp.max(g, axis=-1))
        w_end = jnp.exp(g - m_new[..., None])
        decay = jnp.exp(b_end + m - m_new)
        C_new = decay[..., None, None] * C + jnp.einsum('bhs,bshe,bshd->bhed', w_end, vc, kc)
        n_new = decay[..., None] * n + jnp.einsum('bhs,bshd->bhd', w_end, kc)
        return (C_new, n_new, m_new), jnp.moveaxis(hc, 1, 2)

    carry0 = (C0.astype(f32), n0.astype(f32), m0.astype(f32))
    (C, n, m), hs = lax.scan(step, carry0, xs)
    return jnp.moveaxis(hs, 0, 1).reshape(B, S, H, v.shape[-1]), C, n, m


def mlstm_mixer(h, C0, n0, m0, chunk, w_in, b_gates, head_g, w_out):
    B, S, _ = h.shape
    p = h @ w_in
    q = p[..., :ML_QK].reshape(B, S, ML_HEADS, ML_DK)
    k = p[..., ML_QK:2 * ML_QK].reshape(B, S, ML_HEADS, ML_DK) * (ML_DK ** -0.5)
    v = p[..., 2 * ML_QK:2 * ML_QK + ML_V].reshape(B, S, ML_HEADS, ML_DV)
    o = jax.nn.sigmoid(p[..., 2 * ML_QK + ML_V:2 * ML_QK + ML_V + D_MODEL])
    gates = (p[..., -2 * ML_HEADS:] + b_gates).astype(jnp.float32)
    i_pre = gates[..., :ML_HEADS]
    log_f = jax.nn.log_sigmoid(gates[..., ML_HEADS:])
    hc, C, n, m = mlstm_chunkwise(q, k, v, i_pre, log_f, C0, n0, m0, chunk)
    hn = rmsnorm(hc.astype(h.dtype), head_g).reshape(B, S, D_MODEL)
    return (o * hn) @ w_out, C, n, m


def mem_kv(mem, g, w_kv):
    B, N, _ = mem.shape
    kv = rmsnorm(mem, g) @ w_kv
    return (kv[..., :D_MODEL].reshape(B, N, XA_HEADS, XA_HEAD_DIM),
            kv[..., D_MODEL:].reshape(B, N, XA_HEADS, XA_HEAD_DIM))


def cross_attend(h, k, v, w_q, w_o):
    B, S, _ = h.shape
    q = (h @ w_q).reshape(B, S, XA_HEADS, XA_HEAD_DIM)
    s = jnp.einsum('bshd,bnhd->bhsn', q, k.astype(h.dtype), preferred_element_type=jnp.float32) * XA_SCALE
    pr = jax.nn.softmax(s, axis=-1).astype(h.dtype)
    o = jnp.einsum('bhsn,bnhd->bshd', pr, v.astype(h.dtype)).reshape(B, S, D_MODEL)
    return o @ w_o


def conv_ffn(h, buf, w_gate, w_up, conv_w, conv_b, w_down):
    S = h.shape[1]
    g = h @ w_gate
    u = h @ w_up
    gp = jnp.concatenate([buf.astype(g.dtype), g], axis=1)
    gc = conv_b + conv_w[0] * gp[:, 0:S] + conv_w[1] * gp[:, 1:S + 1] + conv_w[2] * gp[:, 2:S + 2]
    out = (jax.nn.gelu(gc) * u) @ w_down
    return out, gp[:, -(CONV_W - 1):]


def setup_inputs(seed: int = 0) -> dict:
    key = jax.random.key(seed)
    keys = iter(jax.random.split(key, 64))

    def nrm(shape, scale):
        return jax.random.normal(next(keys), shape, jnp.float32) * scale

    D = D_MODEL
    n_pages = PAST_LEN // PAGE_SIZE
    n_pool = (DEC_BATCH * n_pages * 5) // 4
    page_table = jax.random.permutation(next(keys), n_pool)[:DEC_BATCH * n_pages].reshape(DEC_BATCH, n_pages).astype(jnp.int32)
    return {
        'x_prompt': nrm((BATCH, SEQ, D), 1.0),
        'x_sample': nrm((DEC_BATCH, DEC_SEQ, D), 1.0),
        'cache_sb_k': nrm((N_SB, n_pool, PAGE_SIZE, SB_KV_HEADS, SB_HEAD_DIM), 1.0),
        'cache_sb_v': nrm((N_SB, n_pool, PAGE_SIZE, SB_KV_HEADS, SB_HEAD_DIM), 1.0),
        'cache_mla_ckv': nrm((N_MLA, n_pool, PAGE_SIZE, MLA_KV_LORA), 1.0),
        'cache_mla_kr': nrm((N_MLA, n_pool, PAGE_SIZE, MLA_ROPE), 1.0),
        'state_ml_C': nrm((N_ML, DEC_BATCH, ML_HEADS, ML_DV, ML_DK), 0.3),
        'state_ml_n': nrm((N_ML, DEC_BATCH, ML_HEADS, ML_DK), 0.3),
        'state_ml_m': nrm((N_ML, DEC_BATCH, ML_HEADS), 0.5),
        'state_ffn_conv': nrm((DEPTH, DEC_BATCH, CONV_W - 1, D_FF), 1.0),
        'cache_mem_k': nrm((DEPTH, DEC_BATCH, N_MEM, XA_HEADS, XA_HEAD_DIM), 1.0),
        'cache_mem_v': nrm((DEPTH, DEC_BATCH, N_MEM, XA_HEADS, XA_HEAD_DIM), 1.0),
        'page_table': page_table,
        'mem_prompt': nrm((BATCH, N_MEM, D), 1.0),
        'norm_g': 1.0 + nrm((DEPTH, N_NORMS, D), 0.05),
        'sb_w_qkv': nrm((N_SB, D, (SB_HEADS + 2 * SB_KV_HEADS) * SB_HEAD_DIM), D ** -0.5),
        'sb_w_o': nrm((N_SB, SB_HEADS * SB_HEAD_DIM, D), (SB_HEADS * SB_HEAD_DIM) ** -0.5),
        'ml_w_in': nrm((N_ML, D, ML_IN), D ** -0.5),
        'ml_b_gates': jnp.concatenate([nrm((N_ML, ML_HEADS), 0.1), ML_FORGET_BIAS + nrm((N_ML, ML_HEADS), 0.1)], axis=-1),
        'ml_head_g': 1.0 + nrm((N_ML, ML_HEADS, ML_DV), 0.05),
        'ml_w_out': nrm((N_ML, D, D), D ** -0.5),
        'mla_w_dq': nrm((N_MLA, D, MLA_Q_LORA), D ** -0.5),
        'mla_g_q': 1.0 + nrm((N_MLA, MLA_Q_LORA), 0.05),
        'mla_w_uq': nrm((N_MLA, MLA_Q_LORA, MLA_HEADS * (MLA_NOPE + MLA_ROPE)), MLA_Q_LORA ** -0.5),
        'mla_w_dkv': nrm((N_MLA, D, MLA_KV_LORA + MLA_ROPE), D ** -0.5),
        'mla_g_kv': 1.0 + nrm((N_MLA, MLA_KV_LORA), 0.05),
        'mla_w_uk': nrm((N_MLA, MLA_KV_LORA, MLA_HEADS, MLA_NOPE), MLA_KV_LORA ** -0.5),
        'mla_w_uv': nrm((N_MLA, MLA_KV_LORA, MLA_HEADS, MLA_VDIM), MLA_KV_LORA ** -0.5),
        'mla_w_o': nrm((N_MLA, MLA_HEADS * MLA_VDIM, D), (MLA_HEADS * MLA_VDIM) ** -0.5),
        'xa_w_q': nrm((DEPTH, D, D), D ** -0.5),
        'xa_w_kv': nrm((DEPTH, D, 2 * D), D ** -0.5),
        'xa_w_o': nrm((DEPTH, D, D), D ** -0.5),
        'ffn_w_gate': nrm((DEPTH, D, D_FF), D ** -0.5),
        'ffn_w_up': nrm((DEPTH, D, D_FF), D ** -0.5),
        'ffn_conv_w': nrm((DEPTH, CONV_W, D_FF), CONV_W ** -0.5),
        'ffn_conv_b': nrm((DEPTH, D_FF), 0.01),
        'ffn_w_down': nrm((DEPTH, D_FF, D), D_FF ** -0.5),
    }


def reference(x_prompt, x_sample, cache_sb_k, cache_sb_v, cache_mla_ckv, cache_mla_kr,
              state_ml_C, state_ml_n, state_ml_m, state_ffn_conv, cache_mem_k, cache_mem_v,
              page_table, mem_prompt, norm_g, sb_w_qkv, sb_w_o,
              ml_w_in, ml_b_gates, ml_head_g, ml_w_out,
              mla_w_dq, mla_g_q, mla_w_uq, mla_w_dkv, mla_g_kv, mla_w_uk, mla_w_uv, mla_w_o,
              xa_w_q, xa_w_kv, xa_w_o, ffn_w_gate, ffn_w_up, ffn_conv_w, ffn_conv_b, ffn_w_down):
    Bp, Sp, _ = x_prompt.shape
    Bs, Ss, _ = x_sample.shape
    past_len = page_table.shape[1] * cache_sb_k.shape[2]
    pos_p = jnp.arange(Sp, dtype=jnp.int32)
    pos_s = past_len + jnp.arange(Ss, dtype=jnp.int32)
    pos_past = jnp.arange(past_len, dtype=jnp.int32)

    p_sb_k, p_sb_v, p_ckv, p_kr, p_C, p_n, p_m, p_conv, p_mk, p_mv = [], [], [], [], [], [], [], [], [], []
    s_sb_k, s_sb_v, s_ckv, s_kr, s_C, s_n, s_m, s_conv = [], [], [], [], [], [], [], []

    xp, xs = x_prompt, x_sample
    for i in range(DEPTH):
        kind, j = i % N_MIXERS, i // N_MIXERS
        g = norm_g[i]
        hp, hs = rmsnorm(xp, g[0]), rmsnorm(xs, g[0])
        if kind == 0:
            mp, kp, vp = sb_mixer(hp, pos_p, [], sb_w_qkv[j], sb_w_o[j])
            past = [(gather_pages(cache_sb_k, j, page_table), gather_pages(cache_sb_v, j, page_table), pos_past)]
            ms, ks_, vs_ = sb_mixer(hs, pos_s, past, sb_w_qkv[j], sb_w_o[j])
            p_sb_k.append(kp); p_sb_v.append(vp); s_sb_k.append(ks_); s_sb_v.append(vs_)
        elif kind == 1:
            z_C = jnp.zeros((Bp, ML_HEADS, ML_DV, ML_DK), jnp.float32)
            z_n = jnp.zeros((Bp, ML_HEADS, ML_DK), jnp.float32)
            z_m = jnp.zeros((Bp, ML_HEADS), jnp.float32)
            mp, Cp, np_, mp_ = mlstm_mixer(hp, z_C, z_n, z_m, min(ML_CHUNK, Sp), ml_w_in[j], ml_b_gates[j], ml_head_g[j], ml_w_out[j])
            ms, Cs, ns_, ms_ = mlstm_mixer(hs, state_ml_C[j], state_ml_n[j], state_ml_m[j], Ss, ml_w_in[j], ml_b_gates[j], ml_head_g[j], ml_w_out[j])
            p_C.append(Cp); p_n.append(np_); p_m.append(mp_)
            s_C.append(Cs); s_n.append(ns_); s_m.append(ms_)
        else:
            wts = (mla_w_dq[j], mla_g_q[j], mla_w_uq[j], mla_w_dkv[j], mla_g_kv[j], mla_w_uk[j], mla_w_uv[j], mla_w_o[j])
            mp, cp, rp = mla_mixer(hp, pos_p, [], *wts)
            past = [(gather_pages(cache_mla_ckv, j, page_table), gather_pages(cache_mla_kr, j, page_table), pos_past)]
            ms, cs, rs = mla_mixer(hs, pos_s, past, *wts)
            p_ckv.append(cp); p_kr.append(rp); s_ckv.append(cs); s_kr.append(rs)
        xp = xp + rmsnorm(mp, g[1])
        xs = xs + rmsnorm(ms, g[1])
        mk, mv = mem_kv(mem_prompt, g[2], xa_w_kv[i])
        p_mk.append(mk); p_mv.append(mv)
        xp = xp + rmsnorm(cross_attend(rmsnorm(xp, g[3]), mk, mv, xa_w_q[i], xa_w_o[i]), g[4])
        xs = xs + rmsnorm(cross_attend(rmsnorm(xs, g[3]), cache_mem_k[i], cache_mem_v[i], xa_w_q[i], xa_w_o[i]), g[4])
        zero_buf = jnp.zeros((Bp, CONV_W - 1, D_FF), xp.dtype)
        fp, bp = conv_ffn(rmsnorm(xp, g[5]), zero_buf, ffn_w_gate[i], ffn_w_up[i], ffn_conv_w[i], ffn_conv_b[i], ffn_w_down[i])
        fs, bs = conv_ffn(rmsnorm(xs, g[5]), state_ffn_conv[i], ffn_w_gate[i], ffn_w_up[i], ffn_conv_w[i], ffn_conv_b[i], ffn_w_down[i])
        p_conv.append(bp); s_conv.append(bs)
        xp = xp + rmsnorm(fp, g[6])
        xs = xs + rmsnorm(fs, g[6])

    return (xp, xs,
            jnp.stack(p_sb_k), jnp.stack(p_sb_v), jnp.stack(p_ckv), jnp.stack(p_kr),
            jnp.stack(p_C), jnp.stack(p_n), jnp.stack(p_m), jnp.stack(p_conv),
            jnp.stack(p_mk), jnp.stack(p_mv),
            jnp.stack(s_sb_k), jnp.stack(s_sb_v), jnp.stack(s_ckv), jnp.stack(s_kr),
            jnp.stack(s_C), jnp.stack(s_n), jnp.stack(s_m), jnp.stack(s_conv))
```

```python
import functools

import jax
import jax.numpy as jnp
from jax import lax
from jax.experimental import pallas as pl
from jax.experimental.pallas import tpu as pltpu

F32 = jnp.float32
BF16 = jnp.bfloat16

D_MODEL = 1024
EPS = 1e-6
N_MIXERS = 3
PAGE = 128

SB_HEADS = 16
SB_KV_HEADS = 4
SB_GROUP = SB_HEADS // SB_KV_HEADS
SB_HEAD_DIM = D_MODEL // SB_HEADS
SB_SCALE = SB_HEAD_DIM ** -0.5
SB_EXIT = 60.0

ML_HEADS = 4
ML_DV = D_MODEL // ML_HEADS
ML_DK = ML_DV // 2
ML_QK = ML_HEADS * ML_DK
ML_CHUNK = 128

MLA_HEADS = 8
MLA_NOPE = 128
MLA_ROPE = 64
MLA_VDIM = 128
MLA_KV_LORA = 512
MLA_SCALE = (MLA_NOPE + MLA_ROPE) ** -0.5
ROPE_THETA = 10000.0

N_MEM = 256
XA_HEADS = 4
XA_HEAD_DIM = D_MODEL // XA_HEADS
XA_SCALE = XA_HEAD_DIM ** -0.5

D_FF = 2816
FFN_CHUNK = 256

NEG_BIG = -1e30
VMEM_LIMIT_BYTES = 52 * 1024 * 1024


def _cparams(*sem):
    return pltpu.CompilerParams(dimension_semantics=sem, vmem_limit_bytes=VMEM_LIMIT_BYTES)


def _rms(x, g_row):
    ms = jnp.mean(x * x, axis=-1, keepdims=True)
    return x * lax.rsqrt(ms + EPS) * g_row


def _dot(a, b):
    return jnp.dot(a, b, preferred_element_type=F32)


def _dot_nt(a, b):
    return lax.dot_general(a, b, (((1,), (1,)), ((), ())), preferred_element_type=F32)


def _dot_tn(a, b):
    return lax.dot_general(a, b, (((0,), (0,)), ((), ())), preferred_element_type=F32)


def _row_tile(m, pref):
    t = min(pref, m)
    assert m % t == 0, (m, t)
    return t


def _norm_proj_kernel(plan, x_ref, g_ref, *refs):
    n_in = sum(1 + (3 if o["rope"] else 0) + (1 if o["post"] else 0) for o in plan)
    in_refs, out_refs = refs[:n_in], refs[n_in:]
    xn = _rms(x_ref[...], g_ref[...]).astype(BF16)
    p = 0
    for o, o_ref in zip(plan, out_refs):
        y = _dot(xn, in_refs[p][...])
        p += 1
        if o["rope"]:
            y = y * in_refs[p + 1][...] + _dot(xn, in_refs[p][...]) * in_refs[p + 2][...]
            p += 3
        if o["post"]:
            y = _rms(y, in_refs[p][...])
            p += 1
        if o["scale"] != 1.0:
            y = y * o["scale"]
        o_ref[...] = y.astype(o_ref.dtype)


def norm_proj(x, g, outs, tm_pref=512):
    m, k = x.shape
    tm = _row_tile(m, tm_pref)
    plan, args, in_specs, out_shapes, out_specs = [], [x, g.reshape(1, k)], [], [], []
    in_specs.append(pl.BlockSpec((tm, k), lambda i: (i, 0)))
    in_specs.append(pl.BlockSpec((1, k), lambda i: (0, 0)))
    for o in outs:
        w = o["w"]
        n = w.shape[1]
        rope, post = o.get("rope"), o.get("post")
        plan.append(dict(rope=rope is not None, post=post is not None, scale=float(o.get("scale", 1.0))))
        args.append(w)
        in_specs.append(pl.BlockSpec((k, n), lambda i: (0, 0)))
        if rope is not None:
            w_sw, cos, sin = rope
            period = cos.shape[0] // tm
            assert cos.shape[0] % tm == 0
            args += [w_sw, cos, sin]
            in_specs.append(pl.BlockSpec((k, n), lambda i: (0, 0)))
            in_specs.append(pl.BlockSpec((tm, n), lambda i, period=period: (i % period, 0)))
            in_specs.append(pl.BlockSpec((tm, n), lambda i, period=period: (i % period, 0)))
        if post is not None:
            args.append(post.reshape(1, n))
            in_specs.append(pl.BlockSpec((1, n), lambda i: (0, 0)))
        out_shapes.append(jax.ShapeDtypeStruct((m, n), o.get("dtype", F32)))
        out_specs.append(pl.BlockSpec((tm, n), lambda i: (i, 0)))
    return pl.pallas_call(
        functools.partial(_norm_proj_kernel, plan),
        grid=(m // tm,),
        in_specs=in_specs,
        out_specs=out_specs,
        out_shape=out_shapes,
        compiler_params=_cparams("parallel"),
        name="norm_proj",
    )(*args)


def _proj_norm_res_kernel(h_ref, w_ref, g_ref, x_ref, o_ref):
    y = _dot(h_ref[...].astype(BF16), w_ref[...])
    o_ref[...] = x_ref[...] + _rms(y, g_ref[...])


def proj_norm_res(h, w, g, x, tm_pref=512):
    m, k = h.shape
    d = w.shape[1]
    tm = _row_tile(m, tm_pref)
    return pl.pallas_call(
        _proj_norm_res_kernel,
        grid=(m // tm,),
        in_specs=[
            pl.BlockSpec((tm, k), lambda i: (i, 0)),
            pl.BlockSpec((k, d), lambda i: (0, 0)),
            pl.BlockSpec((1, d), lambda i: (0, 0)),
            pl.BlockSpec((tm, d), lambda i: (i, 0)),
        ],
        out_specs=pl.BlockSpec((tm, d), lambda i: (i, 0)),
        out_shape=jax.ShapeDtypeStruct((m, d), F32),
        compiler_params=_cparams("parallel"),
        name="proj_norm_res",
    )(h, w, g.reshape(1, d), x)


def _gelu_tanh(x):
    c = 0.7978845608028654
    return 0.5 * x * (1.0 + jnp.tanh(c * (x + 0.044715 * (x * x * x))))


def _ffn_kernel(long_seq, tiles_per_seq, nf, x_ref, g5_ref, wg_ref, wu_ref, cw_ref, cb_ref, wd_ref, g6_ref,
                *refs):
    if long_seq:
        state_ref, o_ref, tail_ref, xn_ref, acc_ref, carry_ref = refs
    else:
        p1_ref, p2_ref, o_ref, gfull_ref, xn_ref, acc_ref = refs
    i = pl.program_id(0)
    f = pl.program_id(1)

    @pl.when(f == 0)
    def _():
        xn_ref[...] = _rms(x_ref[...], g5_ref[...]).astype(BF16)
        acc_ref[...] = jnp.zeros_like(acc_ref)

    xn = xn_ref[...]
    g = _dot(xn, wg_ref[...])
    u = _dot(xn, wu_ref[...])
    tm, fc = g.shape
    rows = lax.broadcasted_iota(jnp.int32, (tm, fc), 0)
    r1 = pltpu.roll(g, 1, 0)
    r2 = pltpu.roll(g, 2, 0)
    if long_seq:
        @pl.when(i % tiles_per_seq == 0)
        def _():
            carry_ref[f, 6:8, :] = state_ref[0]

        c0 = carry_ref[f, 6:7, :]
        c1 = carry_ref[f, 7:8, :]
        prev1 = jnp.where(rows == 0, c1, r1)
        prev2 = jnp.where(rows == 0, c0, jnp.where(rows == 1, c1, r2))
        carry_ref[f] = g[tm - 8:, :]
        tail_ref[0] = g[tm - 8:, :]
    else:
        pos = rows % 8
        prev1 = jnp.where(pos == 0, p1_ref[...], r1)
        prev2 = jnp.where(pos <= 1, p2_ref[...], r2)
        gfull_ref[...] = g
    gc = cb_ref[...] + cw_ref[0:1, :] * prev2 + cw_ref[1:2, :] * prev1 + cw_ref[2:3, :] * g
    act = (_gelu_tanh(gc) * u).astype(BF16)
    acc_ref[...] += _dot(act, wd_ref[...])

    @pl.when(f == nf - 1)
    def _():
        o_ref[...] = x_ref[...] + _rms(acc_ref[...], g6_ref[...])


def conv_ffn(x, g5, g6, wg, wu, cw, cb, wd, state, seq_len, tm_pref=1024):
    m, d = x.shape
    ff = wg.shape[1]
    fc = FFN_CHUNK
    nf = ff // fc
    nseq = m // seq_len
    long_seq = seq_len >= 8 and seq_len % 8 == 0 and seq_len > 8
    common_specs = [
        None,
        pl.BlockSpec((1, d), lambda i, f: (0, 0)),
        pl.BlockSpec((d, fc), lambda i, f: (0, f)),
        pl.BlockSpec((d, fc), lambda i, f: (0, f)),
        pl.BlockSpec((3, fc), lambda i, f: (0, f)),
        pl.BlockSpec((1, fc), lambda i, f: (0, f)),
        pl.BlockSpec((fc, d), lambda i, f: (f, 0)),
        pl.BlockSpec((1, d), lambda i, f: (0, 0)),
    ]
    if long_seq:
        tm = _row_tile(seq_len, tm_pref)
        tps = seq_len // tm
        nm = m // tm
        common_specs[0] = pl.BlockSpec((tm, d), lambda i, f: (i, 0))
        in_specs = common_specs + [pl.BlockSpec((1, 2, fc), lambda i, f: (i // tps, 0, f))]
        out_specs = [pl.BlockSpec((tm, d), lambda i, f: (i, 0)),
                     pl.BlockSpec((1, 8, fc), lambda i, f: (i, 0, f))]
        out_shape = [jax.ShapeDtypeStruct((m, d), F32), jax.ShapeDtypeStruct((nm, 8, ff), F32)]
        scratch = [pltpu.VMEM((tm, d), BF16), pltpu.VMEM((tm, d), F32), pltpu.VMEM((nf, 8, fc), F32)]
        kern = functools.partial(_ffn_kernel, True, tps, nf)
        args = (x, g5.reshape(1, d), wg, wu, cw, cb.reshape(1, ff), wd, g6.reshape(1, d), state)
        y, tails = pl.pallas_call(
            kern, grid=(nm, nf), in_specs=in_specs, out_specs=out_specs, out_shape=out_shape,
            scratch_shapes=scratch, compiler_params=_cparams("arbitrary", "arbitrary"), name="conv_ffn_long",
        )(*args)
        new_state = tails.reshape(nseq, tps, 8, ff)[:, -1, 6:8, :]
        return y, new_state
    assert seq_len == 8
    tm = _row_tile(m, tm_pref)
    nm = m // tm
    zeros = jnp.zeros((nseq, 8, ff), F32)
    p1 = zeros.at[:, 0].set(state[:, 1]).reshape(m, ff)
    p2 = zeros.at[:, 0].set(state[:, 0]).at[:, 1].set(state[:, 1]).reshape(m, ff)
    common_specs[0] = pl.BlockSpec((tm, d), lambda i, f: (i, 0))
    in_specs = common_specs + [pl.BlockSpec((tm, fc), lambda i, f: (i, f)), pl.BlockSpec((tm, fc), lambda i, f: (i, f))]
    out_specs = [pl.BlockSpec((tm, d), lambda i, f: (i, 0)), pl.BlockSpec((tm, fc), lambda i, f: (i, f))]
    out_shape = [jax.ShapeDtypeStruct((m, d), F32), jax.ShapeDtypeStruct((m, ff), F32)]
    scratch = [pltpu.VMEM((tm, d), BF16), pltpu.VMEM((tm, d), F32)]
    kern = functools.partial(_ffn_kernel, False, 1, nf)
    y, gfull = pl.pallas_call(
        kern, grid=(nm, nf), in_specs=in_specs, out_specs=out_specs, out_shape=out_shape,
        scratch_shapes=scratch, compiler_params=_cparams("parallel", "arbitrary"), name="conv_ffn_short",
    )(x, g5.reshape(1, d), wg, wu, cw, cb.reshape(1, ff), wd, g6.reshape(1, d), p1, p2)
    new_state = gfull.reshape(nseq, 8, ff)[:, 6:8, :]
    return y, new_state


def swap_halves(w, width):
    k, n = w.shape
    half = width // 2
    w3 = w.reshape(k, n // width, width)
    return jnp.concatenate([w3[..., half:], w3[..., :half]], axis=-1).reshape(k, n)


def rope_tables(pos, n_heads):
    half = MLA_ROPE // 2
    inv = ROPE_THETA ** (-jnp.arange(half, dtype=F32) / half)
    ang = pos.astype(F32)[:, None] * inv[None, :]
    cos, sin = jnp.cos(ang), jnp.sin(ang)
    cos2 = jnp.concatenate([cos, cos], axis=-1)
    sin2 = jnp.concatenate([-sin, sin], axis=-1)
    return jnp.tile(cos2, (1, n_heads)), jnp.tile(sin2, (1, n_heads))


def _xattn_kernel(q_ref, k_ref, v_ref, o_ref):
    hd = XA_HEAD_DIM
    for h in range(XA_HEADS):
        sl = slice(h * hd, (h + 1) * hd)
        qh = q_ref[:, sl].astype(BF16)
        kh = k_ref[0, 0, :, h, :].astype(BF16)
        vh = v_ref[0, 0, :, h, :].astype(BF16)
        s = _dot_nt(qh, kh)
        p = jnp.exp(s - jnp.max(s, axis=-1, keepdims=True))
        l = jnp.sum(p, axis=-1, keepdims=True)
        o_ref[:, sl] = (_dot(p.astype(BF16), vh) / l).astype(o_ref.dtype)


def xattn_core(q, k, v, layer, seq_len, tq_pref=512):
    m, d = q.shape
    _, _, n, nh, hd = k.shape
    tq = _row_tile(seq_len, tq_pref)
    tpb = seq_len // tq
    return pl.pallas_call(
        _xattn_kernel,
        grid=(m // tq,),
        in_specs=[
            pl.BlockSpec((tq, d), lambda i: (i, 0)),
            pl.BlockSpec((1, 1, n, nh, hd), lambda i: (layer, i // tpb, 0, 0, 0)),
            pl.BlockSpec((1, 1, n, nh, hd), lambda i: (layer, i // tpb, 0, 0, 0)),
        ],
        out_specs=pl.BlockSpec((tq, d), lambda i: (i, 0)),
        out_shape=jax.ShapeDtypeStruct((m, d), q.dtype),
        compiler_params=_cparams("parallel"),
        name="xattn_core",
    )(q, k, v)


def _sb_tri():
    j = lax.broadcasted_iota(jnp.int32, (PAGE, 2 * PAGE), 0)
    s = lax.broadcasted_iota(jnp.int32, (PAGE, 2 * PAGE), 1)
    return jnp.where((s >= PAGE) | (j > s), 1.0, 0.0).astype(BF16)


def _sb_block(z, valid, r, tri):
    lsz = jnp.minimum(z, 0.0) - jnp.log(1.0 + jnp.exp(-jnp.abs(z)))
    ls = lsz - z
    if valid is not None:
        ls = jnp.where(valid, ls, 0.0)
    hi = ls.astype(BF16)
    lo = (ls - hi.astype(F32)).astype(BF16)
    cum = _dot(hi, tri) + _dot(lo, tri)
    a = jnp.exp(lsz + cum[:, :PAGE] + r)
    if valid is not None:
        a = jnp.where(valid, a, 0.0)
    return a.astype(BF16), r + cum[:, PAGE:]


def _sb_prompt_kernel(q_ref, k_ref, v_ref, o_ref, kh_ref, vh_ref, qs_ref, r_ref, acc_ref):
    qi = pl.program_id(1)
    hd, grp, nkv = SB_HEAD_DIM, SB_GROUP, SB_KV_HEADS
    tq = q_ref.shape[0]

    @pl.when(qi == 0)
    def _():
        for h in range(nkv):
            kh_ref[h] = k_ref[:, h * hd:(h + 1) * hd].astype(BF16)
            vh_ref[h] = v_ref[:, h * hd:(h + 1) * hd].astype(BF16)

    for h in range(nkv):
        for g in range(grp):
            c = (h * grp + g) * hd
            qs_ref[h, g * tq:(g + 1) * tq, :] = q_ref[:, c:c + hd].astype(BF16)
    r_ref[...] = jnp.zeros_like(r_ref)
    acc_ref[...] = jnp.zeros_like(acc_ref)
    tri = _sb_tri()
    rows_t = lax.broadcasted_iota(jnp.int32, (grp * tq, PAGE), 0) % tq
    cols = lax.broadcasted_iota(jnp.int32, (grp * tq, PAGE), 1)

    def cond(c):
        j, rmax = c
        return jnp.logical_and(j <= qi, rmax > -SB_EXIT)

    def body(c):
        j, _ = c
        start = pl.multiple_of((qi - j) * PAGE, PAGE)
        valid = cols < rows_t + jnp.where(j > 0, PAGE, 0)
        rmax = jnp.float32(-jnp.inf)
        for h in range(nkv):
            a, rn = _sb_block(_dot_nt(qs_ref[h], kh_ref[h, pl.ds(start, PAGE), :]), valid, r_ref[h], tri)
            acc_ref[h] += _dot(a, vh_ref[h, pl.ds(start, PAGE), :])
            r_ref[h] = rn
            rmax = jnp.maximum(rmax, jnp.max(rn))
        return j + 1, rmax

    lax.while_loop(cond, body, (jnp.int32(0), jnp.float32(0.0)))
    o_ref[...] = jnp.concatenate(
        [acc_ref[h, g * tq:(g + 1) * tq, :] for h in range(nkv) for g in range(grp)], axis=-1).astype(o_ref.dtype)


def sb_attention_prompt(q, k, v, seq_len):
    m = q.shape[0]
    tq = PAGE
    nq = seq_len // tq
    nb = m // seq_len
    nkv, hd, grp = SB_KV_HEADS, SB_HEAD_DIM, SB_GROUP
    return pl.pallas_call(
        _sb_prompt_kernel,
        grid=(nb, nq),
        in_specs=[
            pl.BlockSpec((tq, SB_HEADS * hd), lambda b, i: (b * nq + i, 0)),
            pl.BlockSpec((seq_len, nkv * hd), lambda b, i: (b, 0)),
            pl.BlockSpec((seq_len, nkv * hd), lambda b, i: (b, 0)),
        ],
        out_specs=pl.BlockSpec((tq, SB_HEADS * hd), lambda b, i: (b * nq + i, 0)),
        out_shape=jax.ShapeDtypeStruct((m, SB_HEADS * hd), BF16),
        scratch_shapes=[
            pltpu.VMEM((nkv, seq_len, hd), BF16),
            pltpu.VMEM((nkv, seq_len, hd), BF16),
            pltpu.VMEM((nkv, grp * tq, hd), BF16),
            pltpu.VMEM((nkv, grp * tq, PAGE), F32),
            pltpu.VMEM((nkv, grp * tq, hd), F32),
        ],
        compiler_params=_cparams("parallel", "arbitrary"),
        name="sb_attention_prompt",
    )(q, k, v)


def _sb_sample_kernel(layer, n_pages, pt_ref, q_ref, kn_ref, vn_ref, kpool, vpool, o_ref,
                      kbuf, vbuf, sem, kfb, vfb, semfb, qs_ref, r_ref, acc_ref, knp_ref, vnp_ref):
    b = pl.program_id(0)
    nb = pl.num_programs(0)
    hd, grp, nkv = SB_HEAD_DIM, SB_GROUP, SB_KV_HEADS
    sq = q_ref.shape[0]
    last = n_pages - 1

    def newest_page(bb, slot):
        page = pt_ref[bb, last]
        return (pltpu.make_async_copy(kpool.at[layer, page], kbuf.at[slot], sem.at[slot, 0]),
                pltpu.make_async_copy(vpool.at[layer, page], vbuf.at[slot], sem.at[slot, 1]))

    @pl.when(b == 0)
    def _():
        for c in newest_page(0, 0):
            c.start()

    @pl.when(b + 1 < nb)
    def _():
        for c in newest_page(b + 1, (b + 1) % 2):
            c.start()

    for h in range(nkv):
        for g in range(grp):
            c = (h * grp + g) * hd
            qs_ref[h, g * sq:(g + 1) * sq, :] = q_ref[:, c:c + hd]
    r_ref[...] = jnp.zeros_like(r_ref)
    acc_ref[...] = jnp.zeros_like(acc_ref)
    knp_ref[...] = jnp.zeros_like(knp_ref)
    vnp_ref[...] = jnp.zeros_like(vnp_ref)
    knp_ref[0:sq, :] = kn_ref[...]
    vnp_ref[0:sq, :] = vn_ref[...]
    tri = _sb_tri()
    rows_t = lax.broadcasted_iota(jnp.int32, (grp * sq, PAGE), 0) % sq
    cols = lax.broadcasted_iota(jnp.int32, (grp * sq, PAGE), 1)

    def sweep(logits, weighted_values, valid):
        rmax = jnp.float32(-jnp.inf)
        for h in range(nkv):
            a, rn = _sb_block(logits(h, qs_ref[h].astype(BF16)), valid, r_ref[h], tri)
            acc_ref[h] += weighted_values(h, a)
            r_ref[h] = rn
            rmax = jnp.maximum(rmax, jnp.max(rn))
        return rmax

    def sweep_page(k_ref, v_ref):
        return sweep(lambda h, qs: _dot(qs, k_ref[h].astype(BF16)),
                     lambda h, a: _dot_nt(a, v_ref[h].astype(BF16)), None)

    sweep(lambda h, qs: _dot_nt(qs, knp_ref[:, h * hd:(h + 1) * hd].astype(BF16)),
          lambda h, a: _dot(a, vnp_ref[:, h * hd:(h + 1) * hd].astype(BF16)), cols < rows_t)
    slot = b % 2
    for c in newest_page(b, slot):
        c.wait()
    rmax = sweep_page(kbuf.at[slot], vbuf.at[slot])

    def cond(c):
        p, rm = c
        return jnp.logical_and(p >= 0, rm > -SB_EXIT)

    def body(c):
        p, _ = c
        page = pt_ref[b, p]
        ck = pltpu.make_async_copy(kpool.at[layer, page], kfb, semfb.at[0])
        cv = pltpu.make_async_copy(vpool.at[layer, page], vfb, semfb.at[1])
        ck.start()
        cv.start()
        ck.wait()
        cv.wait()
        return p - 1, sweep_page(kfb, vfb)

    lax.while_loop(cond, body, (jnp.int32(last - 1), rmax))
    o_ref[...] = jnp.concatenate(
        [acc_ref[h, g * sq:(g + 1) * sq, :] for h in range(nkv) for g in range(grp)], axis=-1).astype(o_ref.dtype)


def sb_attention_sample(q, k_new, v_new, k_pool, v_pool, layer, page_table, seq_len):
    m = q.shape[0]
    nb, n_pages = page_table.shape
    assert m == nb * seq_len and seq_len == 8
    nkv, hd, grp = SB_KV_HEADS, SB_HEAD_DIM, SB_GROUP
    grid_spec = pltpu.PrefetchScalarGridSpec(
        num_scalar_prefetch=1,
        grid=(nb,),
        in_specs=[
            pl.BlockSpec((seq_len, SB_HEADS * hd), lambda b, pt: (b, 0)),
            pl.BlockSpec((seq_len, nkv * hd), lambda b, pt: (b, 0)),
            pl.BlockSpec((seq_len, nkv * hd), lambda b, pt: (b, 0)),
            pl.BlockSpec(memory_space=pl.ANY),
            pl.BlockSpec(memory_space=pl.ANY),
        ],
        out_specs=pl.BlockSpec((seq_len, SB_HEADS * hd), lambda b, pt: (b, 0)),
        scratch_shapes=[
            pltpu.VMEM((2, nkv, hd, PAGE), F32),
            pltpu.VMEM((2, nkv, hd, PAGE), F32),
            pltpu.SemaphoreType.DMA((2, 2)),
            pltpu.VMEM((nkv, hd, PAGE), F32),
            pltpu.VMEM((nkv, hd, PAGE), F32),
            pltpu.SemaphoreType.DMA((2,)),
            pltpu.VMEM((nkv, grp * seq_len, hd), F32),
            pltpu.VMEM((nkv, grp * seq_len, PAGE), F32),
            pltpu.VMEM((nkv, grp * seq_len, hd), F32),
            pltpu.VMEM((PAGE, nkv * hd), F32),
            pltpu.VMEM((PAGE, nkv * hd), F32),
        ],
    )
    return pl.pallas_call(
        functools.partial(_sb_sample_kernel, layer, n_pages),
        grid_spec=grid_spec,
        out_shape=jax.ShapeDtypeStruct((m, SB_HEADS * hd), F32),
        compiler_params=_cparams("arbitrary"),
        name="sb_attention_sample",
    )(page_table, q, k_new, v_new, k_pool, v_pool)


def _mla_load_queries(qn_ref, qr_ref, wuk_ref, ql_ref, qr_s_ref):
    tq = qn_ref.shape[0]
    for h in range(MLA_HEADS):
        qn = qn_ref[:, h * MLA_NOPE:(h + 1) * MLA_NOPE].astype(BF16)
        ql_ref[h * tq:(h + 1) * tq, :] = (_dot_nt(qn, wuk_ref[h]) * MLA_SCALE).astype(ql_ref.dtype)
        qr_s_ref[h * tq:(h + 1) * tq, :] = (qr_ref[:, h * MLA_ROPE:(h + 1) * MLA_ROPE] * MLA_SCALE).astype(qr_s_ref.dtype)


def _mla_step(s, ck, valid, m_ref, l_ref, acc_ref):
    if valid is not None:
        s = jnp.where(valid, s, NEG_BIG)
    m_prev = m_ref[...]
    m_new = jnp.maximum(m_prev, jnp.max(s, axis=-1, keepdims=True))
    alpha = jnp.exp(m_prev - m_new)
    p = jnp.exp(s - m_new)
    l_ref[...] = alpha * l_ref[...] + jnp.sum(p, axis=-1, keepdims=True)
    acc_ref[...] = alpha * acc_ref[...] + _dot(p.astype(BF16), ck)
    m_ref[...] = m_new


def _mla_store_output(wuv_ref, l_ref, acc_ref, o_ref):
    tq = o_ref.shape[0]
    for h in range(MLA_HEADS):
        rows = slice(h * tq, (h + 1) * tq)
        ol = (acc_ref[rows, :] / l_ref[rows, :]).astype(BF16)
        o_ref[:, h * MLA_VDIM:(h + 1) * MLA_VDIM] = _dot(ol, wuv_ref[h]).astype(o_ref.dtype)


def _mla_prompt_kernel(tk, qn_ref, qr_ref, ckv_ref, kr_ref, wuk_ref, wuv_ref, o_ref,
                       ck_s, kr_s, ql_s, qr_s, m_s, l_s, acc_s):
    qi = pl.program_id(1)
    tq = qn_ref.shape[0]
    rows_n = MLA_HEADS * tq

    @pl.when(qi == 0)
    def _():
        ck_s[...] = ckv_ref[...].astype(BF16)
        kr_s[...] = kr_ref[...].astype(BF16)

    _mla_load_queries(qn_ref, qr_ref, wuk_ref, ql_s, qr_s)
    m_s[...] = jnp.full_like(m_s, NEG_BIG)
    l_s[...] = jnp.zeros_like(l_s)
    acc_s[...] = jnp.zeros_like(acc_s)
    q_pos = qi * tq + lax.broadcasted_iota(jnp.int32, (rows_n, tk), 0) % tq
    cols = lax.broadcasted_iota(jnp.int32, (rows_n, tk), 1)

    def body(kb, carry):
        start = pl.multiple_of(kb * tk, tk)
        valid = cols + kb * tk <= q_pos
        ck = ck_s[pl.ds(start, tk), :]
        s = _dot_nt(ql_s[...], ck) + _dot_nt(qr_s[...], kr_s[pl.ds(start, tk), :])
        _mla_step(s, ck, valid, m_s, l_s, acc_s)
        return carry

    n_blocks = ((qi + 1) * tq + tk - 1) // tk
    lax.fori_loop(0, n_blocks, body, 0)
    _mla_store_output(wuv_ref, l_s, acc_s, o_ref)


def mla_attention_prompt(qn, qr, ckv, kr, wuk, wuv, seq_len):
    m = qn.shape[0]
    tq = PAGE
    tk = 2 * PAGE if seq_len % (2 * PAGE) == 0 else PAGE
    nq = seq_len // tq
    nb = m // seq_len
    rows_n = MLA_HEADS * tq
    lat = MLA_KV_LORA
    return pl.pallas_call(
        functools.partial(_mla_prompt_kernel, tk),
        grid=(nb, nq),
        in_specs=[
            pl.BlockSpec((tq, MLA_HEADS * MLA_NOPE), lambda b, i: (b * nq + i, 0)),
            pl.BlockSpec((tq, MLA_HEADS * MLA_ROPE), lambda b, i: (b * nq + i, 0)),
            pl.BlockSpec((seq_len, lat), lambda b, i: (b, 0)),
            pl.BlockSpec((seq_len, MLA_ROPE), lambda b, i: (b, 0)),
            pl.BlockSpec((MLA_HEADS, lat, MLA_NOPE), lambda b, i: (0, 0, 0)),
            pl.BlockSpec((MLA_HEADS, lat, MLA_VDIM), lambda b, i: (0, 0, 0)),
        ],
        out_specs=pl.BlockSpec((tq, MLA_HEADS * MLA_VDIM), lambda b, i: (b * nq + i, 0)),
        out_shape=jax.ShapeDtypeStruct((m, MLA_HEADS * MLA_VDIM), BF16),
        scratch_shapes=[
            pltpu.VMEM((seq_len, lat), BF16),
            pltpu.VMEM((seq_len, MLA_ROPE), BF16),
            pltpu.VMEM((rows_n, lat), BF16),
            pltpu.VMEM((rows_n, MLA_ROPE), BF16),
            pltpu.VMEM((rows_n, 1), F32),
            pltpu.VMEM((rows_n, 1), F32),
            pltpu.VMEM((rows_n, lat), F32),
        ],
        compiler_params=_cparams("parallel", "arbitrary"),
        name="mla_attention_prompt",
    )(qn, qr, ckv, kr, wuk, wuv)


def _mla_sample_kernel(layer, n_pages, ppi, pt_ref, qn_ref, qr_ref, cn_ref, rn_ref, cpool, rpool, wuk_ref, wuv_ref, o_ref,
                       cbuf, rbuf, sem, cnp_s, rnp_s, ql_s, qr_s, m_s, l_s, acc_s):
    b = pl.program_id(0)
    nb = pl.num_programs(0)
    sq = qn_ref.shape[0]
    rows_n = MLA_HEADS * sq
    n_it = n_pages // ppi

    def group_copies(bb, it, slot):
        out = []
        for j in range(ppi):
            page = pt_ref[bb, it * ppi + j]
            out.append(pltpu.make_async_copy(cpool.at[layer, page], cbuf.at[slot, pl.ds(j * PAGE, PAGE), :],
                                             sem.at[slot, 0, j]))
            out.append(pltpu.make_async_copy(rpool.at[layer, page], rbuf.at[slot, j], sem.at[slot, 1, j]))
        return out

    @pl.when(b == 0)
    def _():
        for c in group_copies(0, 0, 0):
            c.start()

    _mla_load_queries(qn_ref, qr_ref, wuk_ref, ql_s, qr_s)
    m_s[...] = jnp.full_like(m_s, NEG_BIG)
    l_s[...] = jnp.zeros_like(l_s)
    acc_s[...] = jnp.zeros_like(acc_s)
    cnp_s[...] = jnp.zeros_like(cnp_s)
    rnp_s[...] = jnp.zeros_like(rnp_s)
    cnp_s[0:sq, :] = cn_ref[...]
    rnp_s[0:sq, :] = rn_ref[...]
    rows_t = lax.broadcasted_iota(jnp.int32, (rows_n, PAGE), 0) % sq
    cols = lax.broadcasted_iota(jnp.int32, (rows_n, PAGE), 1)
    ql = ql_s[...].astype(BF16)
    qr = qr_s[...].astype(BF16)
    cn = cnp_s[...].astype(BF16)
    _mla_step(_dot_nt(ql, cn) + _dot_nt(qr, rnp_s[...].astype(BF16)), cn, cols <= rows_t, m_s, l_s, acc_s)

    def body(it, carry):
        step = b * n_it + it
        slot = step % 2

        @pl.when(it + 1 < n_it)
        def _():
            for c in group_copies(b, it + 1, 1 - slot):
                c.start()

        @pl.when(jnp.logical_and(it + 1 == n_it, b + 1 < nb))
        def _():
            for c in group_copies(b + 1, 0, 1 - slot):
                c.start()

        for c in group_copies(b, it, slot):
            c.wait()
        ck = cbuf[slot].astype(BF16)
        s_rope = jnp.concatenate([_dot(qr, rbuf[slot, j].astype(BF16)) for j in range(ppi)], axis=-1)
        _mla_step(_dot_nt(ql, ck) + s_rope, ck, None, m_s, l_s, acc_s)
        return carry

    lax.fori_loop(0, n_it, body, 0)
    _mla_store_output(wuv_ref, l_s, acc_s, o_ref)


def mla_attention_sample(qn, qr, c_new, r_new, c_pool, r_pool, layer, page_table, wuk, wuv, seq_len):
    m = qn.shape[0]
    nb, n_pages = page_table.shape
    assert m == nb * seq_len and seq_len == 8
    ppi = 4 if n_pages % 4 == 0 else 1
    rows_n = MLA_HEADS * seq_len
    lat = MLA_KV_LORA
    grid_spec = pltpu.PrefetchScalarGridSpec(
        num_scalar_prefetch=1,
        grid=(nb,),
        in_specs=[
            pl.BlockSpec((seq_len, MLA_HEADS * MLA_NOPE), lambda b, pt: (b, 0)),
            pl.BlockSpec((seq_len, MLA_HEADS * MLA_ROPE), lambda b, pt: (b, 0)),
            pl.BlockSpec((seq_len, lat), lambda b, pt: (b, 0)),
            pl.BlockSpec((seq_len, MLA_ROPE), lambda b, pt: (b, 0)),
            pl.BlockSpec(memory_space=pl.ANY),
            pl.BlockSpec(memory_space=pl.ANY),
            pl.BlockSpec((MLA_HEADS, lat, MLA_NOPE), lambda b, pt: (0, 0, 0)),
            pl.BlockSpec((MLA_HEADS, lat, MLA_VDIM), lambda b, pt: (0, 0, 0)),
        ],
        out_specs=pl.BlockSpec((seq_len, MLA_HEADS * MLA_VDIM), lambda b, pt: (b, 0)),
        scratch_shapes=[
            pltpu.VMEM((2, ppi * PAGE, lat), F32),
            pltpu.VMEM((2, ppi, MLA_ROPE, PAGE), F32),
            pltpu.SemaphoreType.DMA((2, 2, ppi)),
            pltpu.VMEM((PAGE, lat), F32),
            pltpu.VMEM((PAGE, MLA_ROPE), F32),
            pltpu.VMEM((rows_n, lat), F32),
            pltpu.VMEM((rows_n, MLA_ROPE), F32),
            pltpu.VMEM((rows_n, 1), F32),
            pltpu.VMEM((rows_n, 1), F32),
            pltpu.VMEM((rows_n, lat), F32),
        ],
    )
    return pl.pallas_call(
        functools.partial(_mla_sample_kernel, layer, n_pages, ppi),
        grid_spec=grid_spec,
        out_shape=jax.ShapeDtypeStruct((m, MLA_HEADS * MLA_VDIM), F32),
        compiler_params=_cparams("arbitrary"),
        name="mla_attention_sample",
    )(page_table, qn, qr, c_new, r_new, c_pool, r_pool, wuk, wuv)


def _split3(x):
    a = x.astype(BF16)
    r = x - a.astype(F32)
    b = r.astype(BF16)
    c = (r - b.astype(F32)).astype(BF16)
    return a, b, c


def _mlstm_kernel(lr, nc, q_ref, k_ref, v_ref, og_ref, gt_ref, bias_ref, hg_ref, c0_ref, n0_ref, m0_ref,
                  out_ref, c_out, n_out, m_out, c_s, n_s, m_s, *pad):
    c_idx = pl.program_id(1)
    L = ML_CHUNK
    dk, dv, nh = ML_DK, ML_DV, ML_HEADS

    @pl.when(c_idx == 0)
    def _():
        c_s[...] = c0_ref[0]
        n_s[...] = n0_ref[0]
        m_s[...] = m0_ref[0]

    if lr == L:
        q_all, k_all, v_all, gates = q_ref[...], k_ref[...], v_ref[...], gt_ref[...]
    else:
        qp, kp, vp, gp = pad
        for dst, src in ((qp, q_ref), (kp, k_ref), (vp, v_ref), (gp, gt_ref)):
            dst[...] = jnp.zeros_like(dst)
            dst[0:lr, :] = src[...]
        q_all, k_all, v_all, gates = qp[...], kp[...], vp[...], gp[...]

    gb = gates + bias_ref[...]
    lf = jnp.minimum(gb, 0.0) - jnp.log(1.0 + jnp.exp(-jnp.abs(gb)))
    row = lax.broadcasted_iota(jnp.int32, (L, 128), 0)
    if lr != L:
        lf = jnp.where(row < lr, lf, 0.0)
        gb = jnp.where(row < lr, gb, NEG_BIG)
    li = lax.broadcasted_iota(jnp.int32, (L, L), 0)
    si = lax.broadcasted_iota(jnp.int32, (L, L), 1)
    causal = si <= li
    tril = jnp.where(causal, 1.0, 0.0).astype(BF16)
    t0, t1, t2 = _split3(lf)
    bcum = _dot(tril, t0) + _dot(tril, t1) + _dot(tril, t2)
    bcum_t = bcum.T
    gb_t = gb.T

    for h in range(nh):
        qh = q_all[:, h * dk:(h + 1) * dk]
        kh = k_all[:, h * dk:(h + 1) * dk] * (dk ** -0.5)
        vh = v_all[:, h * dv:(h + 1) * dv]
        qb, kb, vb = qh.astype(BF16), kh.astype(BF16), vh.astype(BF16)
        b_col = bcum[:, nh + h:nh + h + 1]
        i_col = gb[:, h:h + 1]
        b_row = bcum_t[nh + h:nh + h + 1, :]
        i_row = gb_t[h:h + 1, :]
        m_old = m_s[h:h + 1, 0:1]
        n_old = n_s[h:h + 1, :]
        c_old = c_s[h]

        d = jnp.where(causal, b_col - b_row + i_row, NEG_BIG)
        inter = b_col + m_old
        m_q = jnp.maximum(inter, jnp.max(d, axis=-1, keepdims=True))
        sc = _dot_nt(qb, kb) * jnp.exp(d - m_q)
        w_st = jnp.exp(inter - m_q)
        num = _dot(sc.astype(BF16), vb) + w_st * _dot_nt(qb, c_old.astype(BF16))
        den = jnp.sum(sc, axis=-1, keepdims=True) + w_st * jnp.sum(qh * n_old, axis=-1, keepdims=True)
        hc = num / jnp.maximum(jnp.abs(den), jnp.exp(-m_q))

        b_end = bcum[L - 1:L, nh + h:nh + h + 1]
        g_col = b_end - b_col + i_col
        g_row = b_end - b_row + i_row
        m_new = jnp.maximum(b_end + m_old, jnp.max(g_row, axis=-1, keepdims=True))
        w_end = jnp.exp(g_col - m_new)
        decay = jnp.exp(b_end + m_old - m_new)
        c_s[h] = decay * c_old + _dot_tn((vh * w_end).astype(BF16), kb)
        n_s[h:h + 1, :] = decay * n_old + jnp.sum(w_end * kh, axis=0, keepdims=True)
        m_s[h:h + 1, :] = jnp.broadcast_to(m_new, (1, 128))

        hn = _rms(hc, hg_ref[:, h * dv:(h + 1) * dv])
        og = og_ref[...][:, h * dv:(h + 1) * dv]
        gate = 1.0 / (1.0 + jnp.exp(-og))
        out_ref[:, h * dv:(h + 1) * dv] = (gate * hn[0:lr, :]).astype(out_ref.dtype)

    @pl.when(c_idx == nc - 1)
    def _():
        c_out[0] = c_s[...]
        n_out[0] = n_s[...]
        m_out[0] = m_s[...]


def mlstm_core(q, k, v, og, gates, bias, head_g, c0, n0, m0, seq_len, out_dtype):
    m_rows = q.shape[0]
    nb = m_rows // seq_len
    lr = min(ML_CHUNK, seq_len)
    nc = seq_len // lr
    nh, dk, dv = ML_HEADS, ML_DK, ML_DV
    n0p = jnp.zeros((nb, 8, dk), F32).at[:, :nh].set(n0)
    m0p = jnp.zeros((nb, 8, 128), F32).at[:, :nh].set(jnp.broadcast_to(m0[:, :, None], (nb, nh, 128)))
    row_spec = lambda w: pl.BlockSpec((lr, w), lambda b, c: (b * nc + c, 0))
    scratch = [pltpu.VMEM((nh, dv, dk), F32), pltpu.VMEM((8, dk), F32), pltpu.VMEM((8, 128), F32)]
    if lr != ML_CHUNK:
        scratch += [pltpu.VMEM((ML_CHUNK, nh * dk), F32), pltpu.VMEM((ML_CHUNK, nh * dk), F32),
                    pltpu.VMEM((ML_CHUNK, nh * dv), F32), pltpu.VMEM((ML_CHUNK, 128), F32)]
    out, c_f, n_f, m_f = pl.pallas_call(
        functools.partial(_mlstm_kernel, lr, nc),
        grid=(nb, nc),
        in_specs=[
            row_spec(nh * dk), row_spec(nh * dk), row_spec(nh * dv), row_spec(nh * dv), row_spec(128),
            pl.BlockSpec((1, 128), lambda b, c: (0, 0)),
            pl.BlockSpec((1, nh * dv), lambda b, c: (0, 0)),
            pl.BlockSpec((1, nh, dv, dk), lambda b, c: (b, 0, 0, 0)),
            pl.BlockSpec((1, 8, dk), lambda b, c: (b, 0, 0)),
            pl.BlockSpec((1, 8, 128), lambda b, c: (b, 0, 0)),
        ],
        out_specs=[
            row_spec(nh * dv),
            pl.BlockSpec((1, nh, dv, dk), lambda b, c: (b, 0, 0, 0)),
            pl.BlockSpec((1, 8, dk), lambda b, c: (b, 0, 0)),
            pl.BlockSpec((1, 8, 128), lambda b, c: (b, 0, 0)),
        ],
        out_shape=[
            jax.ShapeDtypeStruct((m_rows, nh * dv), out_dtype),
            jax.ShapeDtypeStruct((nb, nh, dv, dk), F32),
            jax.ShapeDtypeStruct((nb, 8, dk), F32),
            jax.ShapeDtypeStruct((nb, 8, 128), F32),
        ],
        scratch_shapes=scratch,
        compiler_params=_cparams("parallel", "arbitrary"),
        name="mlstm_core",
    )(q, k, v, og, gates, bias.reshape(1, 128), head_g.reshape(1, nh * dv), c0, n0p, m0p)
    return out, c_f, n_f[:, :nh], m_f[:, :nh, 0]


def prep_sb(w_qkv, w_o):
    nq, nkv = SB_HEADS * SB_HEAD_DIM, SB_KV_HEADS * SB_HEAD_DIM
    return dict(wq=w_qkv[:, :nq].astype(BF16), wk=w_qkv[:, nq:nq + nkv].astype(BF16),
                wv=w_qkv[:, nq + nkv:].astype(BF16), wo=w_o.astype(BF16))


def sb_mixer(x, g, w, seq_len, past=None):
    q, k, v = norm_proj(x, g, [dict(w=w["wq"], scale=SB_SCALE), dict(w=w["wk"]), dict(w=w["wv"])])
    if past is None:
        o = sb_attention_prompt(q, k, v, seq_len)
    else:
        k_pool, v_pool, layer, page_table = past
        o = sb_attention_sample(q, k, v, k_pool, v_pool, layer, page_table, seq_len)
    return o, k, v


def prep_ml(w_in, b_gates, head_g, w_out):
    o0 = 2 * ML_QK + ML_HEADS * ML_DV
    gates_w = jnp.zeros((D_MODEL, 128), F32).at[:, :2 * ML_HEADS].set(w_in[:, o0 + D_MODEL:])
    bias = jnp.zeros((128,), F32).at[:2 * ML_HEADS].set(b_gates)
    return dict(wq=w_in[:, :ML_QK].astype(BF16), wk=w_in[:, ML_QK:2 * ML_QK].astype(BF16),
                wv=w_in[:, 2 * ML_QK:o0].astype(BF16), wog=w_in[:, o0:o0 + D_MODEL].astype(BF16),
                wgt=gates_w.astype(BF16), bias=bias, head_g=head_g.reshape(-1), wo=w_out.astype(BF16))


def ml_mixer(x, g, w, seq_len, state, out_dtype):
    q, k, v, og, gates = norm_proj(
        x, g, [dict(w=w["wq"]), dict(w=w["wk"]), dict(w=w["wv"]), dict(w=w["wog"]), dict(w=w["wgt"])])
    c0, n0, m0 = state
    return mlstm_core(q, k, v, og, gates, w["bias"], w["head_g"], c0, n0, m0, seq_len, out_dtype)


def prep_mla(w_dq, g_q, w_uq, w_dkv, g_kv, w_uk, w_uv, w_o):
    per = MLA_NOPE + MLA_ROPE
    w_uq3 = w_uq.reshape(w_uq.shape[0], MLA_HEADS, per)
    w_nope = w_uq3[:, :, :MLA_NOPE].reshape(-1, MLA_HEADS * MLA_NOPE)
    w_rope = w_uq3[:, :, MLA_NOPE:].reshape(-1, MLA_HEADS * MLA_ROPE)
    w_c, w_r = w_dkv[:, :MLA_KV_LORA], w_dkv[:, MLA_KV_LORA:]
    return dict(wdq=w_dq.astype(BF16), g_q=g_q, w_nope=w_nope.astype(BF16), w_rope=w_rope.astype(BF16),
                w_rope_sw=swap_halves(w_rope, MLA_ROPE).astype(BF16), w_c=w_c.astype(BF16), g_kv=g_kv,
                w_r=w_r.astype(BF16), w_r_sw=swap_halves(w_r, MLA_ROPE).astype(BF16),
                wuk=jnp.transpose(w_uk, (1, 0, 2)).astype(BF16), wuv=jnp.transpose(w_uv, (1, 0, 2)).astype(BF16),
                wo=w_o.astype(BF16))


def mla_mixer(x, g, w, seq_len, tables, past=None):
    cos1, sin1, cos8, sin8 = tables
    cq, ckv, kr = norm_proj(x, g, [
        dict(w=w["wdq"]),
        dict(w=w["w_c"], post=w["g_kv"]),
        dict(w=w["w_r"], rope=(w["w_r_sw"], cos1, sin1)),
    ], tm_pref=cos1.shape[0] if cos1.shape[0] < 512 else 512)
    qn, qr = norm_proj(cq, w["g_q"], [
        dict(w=w["w_nope"]),
        dict(w=w["w_rope"], rope=(w["w_rope_sw"], cos8, sin8)),
    ], tm_pref=cos1.shape[0] if cos1.shape[0] < 512 else 512)
    if past is None:
        o = mla_attention_prompt(qn, qr, ckv, kr, w["wuk"], w["wuv"], seq_len)
    else:
        c_pool, r_pool, layer, page_table = past
        o = mla_attention_sample(qn, qr, ckv, kr, c_pool, r_pool, layer, page_table, w["wuk"], w["wuv"], seq_len)
    return o, ckv, kr


def kernel(x_prompt, x_sample, cache_sb_k, cache_sb_v, cache_mla_ckv, cache_mla_kr, state_ml_C, state_ml_n, state_ml_m, state_ffn_conv, cache_mem_k, cache_mem_v, page_table, mem_prompt, norm_g, sb_w_qkv, sb_w_o, ml_w_in, ml_b_gates, ml_head_g, ml_w_out, mla_w_dq, mla_g_q, mla_w_uq, mla_w_dkv, mla_g_kv, mla_w_uk, mla_w_uv, mla_w_o, xa_w_q, xa_w_kv, xa_w_o, ffn_w_gate, ffn_w_up, ffn_conv_w, ffn_conv_b, ffn_w_down):
    bp, sp, d = x_prompt.shape
    bs, ss, _ = x_sample.shape
    depth = norm_g.shape[0]
    n_pages = page_table.shape[1]
    past_len = n_pages * cache_sb_k.shape[2]
    n_mem = mem_prompt.shape[1]
    ff = ffn_w_gate.shape[2]
    sb_k_pages = jnp.transpose(cache_sb_k, (0, 1, 3, 4, 2))
    sb_v_pages = jnp.transpose(cache_sb_v, (0, 1, 3, 4, 2))
    mla_kr_pages = jnp.transpose(cache_mla_kr, (0, 1, 3, 2))

    xp = x_prompt.reshape(bp * sp, d)
    xs = x_sample.reshape(bs * ss, d)
    mem = mem_prompt.reshape(bp * n_mem, d)

    pos_p = jnp.arange(sp, dtype=jnp.int32)
    rows_s = min(512, bs * ss)
    pos_s = past_len + jnp.arange(rows_s, dtype=jnp.int32) % ss
    tabs_p = rope_tables(pos_p, 1) + rope_tables(pos_p, MLA_HEADS)
    tabs_s = rope_tables(pos_s, 1) + rope_tables(pos_s, MLA_HEADS)

    outs = {name: [] for name in (
        "p_sb_k", "p_sb_v", "p_ckv", "p_kr", "p_C", "p_n", "p_m", "p_conv", "p_mk", "p_mv",
        "s_sb_k", "s_sb_v", "s_ckv", "s_kr", "s_C", "s_n", "s_m", "s_conv")}

    for i in range(depth):
        kind, j = i % N_MIXERS, i // N_MIXERS
        g = norm_g[i]
        if kind == 0:
            w = prep_sb(sb_w_qkv[j], sb_w_o[j])
            mp, kp, vp = sb_mixer(xp, g[0], w, sp)
            past = (sb_k_pages, sb_v_pages, j, page_table)
            ms, ks_, vs_ = sb_mixer(xs, g[0], w, ss, past)
            outs["p_sb_k"].append(kp.reshape(bp, sp, SB_KV_HEADS, SB_HEAD_DIM))
            outs["p_sb_v"].append(vp.reshape(bp, sp, SB_KV_HEADS, SB_HEAD_DIM))
            outs["s_sb_k"].append(ks_.reshape(bs, ss, SB_KV_HEADS, SB_HEAD_DIM))
            outs["s_sb_v"].append(vs_.reshape(bs, ss, SB_KV_HEADS, SB_HEAD_DIM))
        elif kind == 1:
            w = prep_ml(ml_w_in[j], ml_b_gates[j], ml_head_g[j], ml_w_out[j])
            zero_state = (jnp.zeros((bp, ML_HEADS, ML_DV, ML_DK), F32), jnp.zeros((bp, ML_HEADS, ML_DK), F32),
                          jnp.zeros((bp, ML_HEADS), F32))
            mp, c_p, n_p, m_p = ml_mixer(xp, g[0], w, sp, zero_state, BF16)
            ms, c_s, n_s, m_s = ml_mixer(xs, g[0], w, ss, (state_ml_C[j], state_ml_n[j], state_ml_m[j]), F32)
            outs["p_C"].append(c_p); outs["p_n"].append(n_p); outs["p_m"].append(m_p)
            outs["s_C"].append(c_s); outs["s_n"].append(n_s); outs["s_m"].append(m_s)
        else:
            w = prep_mla(mla_w_dq[j], mla_g_q[j], mla_w_uq[j], mla_w_dkv[j], mla_g_kv[j], mla_w_uk[j],
                         mla_w_uv[j], mla_w_o[j])
            mp, c_p, r_p = mla_mixer(xp, g[0], w, sp, tabs_p)
            ms, c_s, r_s = mla_mixer(xs, g[0], w, ss, tabs_s, past=(cache_mla_ckv, mla_kr_pages, j, page_table))
            outs["p_ckv"].append(c_p.reshape(bp, sp, -1)); outs["p_kr"].append(r_p.reshape(bp, sp, -1))
            outs["s_ckv"].append(c_s.reshape(bs, ss, -1)); outs["s_kr"].append(r_s.reshape(bs, ss, -1))
        xp = proj_norm_res(mp, w["wo"], g[1], xp)
        xs = proj_norm_res(ms, w["wo"], g[1], xs)

        w_kv = xa_w_kv[i]
        mk, mv = norm_proj(mem, g[2], [dict(w=w_kv[:, :d].astype(BF16)), dict(w=w_kv[:, d:].astype(BF16))])
        mk = mk.reshape(1, bp, n_mem, XA_HEADS, XA_HEAD_DIM)
        mv = mv.reshape(1, bp, n_mem, XA_HEADS, XA_HEAD_DIM)
        outs["p_mk"].append(mk[0])
        outs["p_mv"].append(mv[0])
        wq, wo = xa_w_q[i].astype(BF16), xa_w_o[i].astype(BF16)
        (qp,) = norm_proj(xp, g[3], [dict(w=wq, scale=XA_SCALE, dtype=BF16)])
        ap = xattn_core(qp, mk, mv, 0, sp)
        xp = proj_norm_res(ap, wo, g[4], xp)
        (qs,) = norm_proj(xs, g[3], [dict(w=wq, scale=XA_SCALE)])
        as_ = xattn_core(qs, cache_mem_k, cache_mem_v, i, ss)
        xs = proj_norm_res(as_, wo, g[4], xs)

        wg, wu, wd = ffn_w_gate[i].astype(BF16), ffn_w_up[i].astype(BF16), ffn_w_down[i].astype(BF16)
        xp, conv_p = conv_ffn(xp, g[5], g[6], wg, wu, ffn_conv_w[i], ffn_conv_b[i], wd,
                              jnp.zeros((bp, 2, ff), F32), sp)
        xs, conv_s = conv_ffn(xs, g[5], g[6], wg, wu, ffn_conv_w[i], ffn_conv_b[i], wd, state_ffn_conv[i], ss)
        outs["p_conv"].append(conv_p); outs["s_conv"].append(conv_s)

    st = {name: jnp.stack(v) for name, v in outs.items()}
    return (xp.reshape(bp, sp, d), xs.reshape(bs, ss, d),
            st["p_sb_k"], st["p_sb_v"], st["p_ckv"], st["p_kr"], st["p_C"], st["p_n"], st["p_m"], st["p_conv"],
            st["p_mk"], st["p_mv"],
            st["s_sb_k"], st["s_sb_v"], st["s_ckv"], st["s_kr"], st["s_C"], st["s_n"], st["s_m"], st["s_conv"])
```

```python
import functools

import jax
import jax.numpy as jnp
from jax import lax
from jax.experimental import pallas as pl
from jax.experimental.pallas import tpu as pltpu

F32 = jnp.float32
BF16 = jnp.bfloat16

D_MODEL = 1024
EPS = 1e-6
N_MIXERS = 3
PAGE = 128

SB_HEADS = 16
SB_KV_HEADS = 4
SB_GROUP = SB_HEADS // SB_KV_HEADS
SB_HEAD_DIM = D_MODEL // SB_HEADS
SB_SCALE = SB_HEAD_DIM ** -0.5
SB_EXIT = 60.0

ML_HEADS = 4
ML_DV = D_MODEL // ML_HEADS
ML_DK = ML_DV // 2
ML_QK = ML_HEADS * ML_DK
ML_CHUNK = 128

MLA_HEADS = 8
MLA_NOPE = 128
MLA_ROPE = 64
MLA_VDIM = 128
MLA_KV_LORA = 512
MLA_SCALE = (MLA_NOPE + MLA_ROPE) ** -0.5
MLA_GROUP_HEADS = 2
MLA_PAGES_PER_GROUP = 8
ROPE_THETA = 10000.0

N_MEM = 256
XA_HEADS = 4
XA_HEAD_DIM = D_MODEL // XA_HEADS
XA_SCALE = XA_HEAD_DIM ** -0.5

D_FF = 2816
FFN_CHUNK = 256

NEG_BIG = -1e30
VMEM_LIMIT_BYTES = 52 * 1024 * 1024


def _cparams(*sem):
    return pltpu.CompilerParams(dimension_semantics=sem, vmem_limit_bytes=VMEM_LIMIT_BYTES)


def _rms(x, g_row):
    ms = jnp.mean(x * x, axis=-1, keepdims=True)
    return x * lax.rsqrt(ms + EPS) * g_row


def _dot(a, b):
    return jnp.dot(a, b, preferred_element_type=F32)


def _dot_nt(a, b):
    return lax.dot_general(a, b, (((1,), (1,)), ((), ())), preferred_element_type=F32)


def _dot_tn(a, b):
    return lax.dot_general(a, b, (((0,), (0,)), ((), ())), preferred_element_type=F32)


def _row_tile(m, pref):
    t = min(pref, m)
    assert m % t == 0, (m, t)
    return t


def _norm_proj_kernel(plan, x_ref, g_ref, *refs):
    n_in = sum(1 + (3 if o["rope"] else 0) + (1 if o["post"] else 0) for o in plan)
    in_refs, out_refs = refs[:n_in], refs[n_in:]
    xn = _rms(x_ref[...], g_ref[...]).astype(BF16)
    p = 0
    for o, o_ref in zip(plan, out_refs):
        y = _dot(xn, in_refs[p][...])
        p += 1
        if o["rope"]:
            y = y * in_refs[p + 1][...] + _dot(xn, in_refs[p][...]) * in_refs[p + 2][...]
            p += 3
        if o["post"]:
            y = _rms(y, in_refs[p][...])
            p += 1
        if o["scale"] != 1.0:
            y = y * o["scale"]
        o_ref[...] = y.astype(o_ref.dtype)


def norm_proj(x, g, outs, tm_pref=512):
    m, k = x.shape
    tm = _row_tile(m, tm_pref)
    plan, args, in_specs, out_shapes, out_specs = [], [x, g.reshape(1, k)], [], [], []
    in_specs.append(pl.BlockSpec((tm, k), lambda i: (i, 0)))
    in_specs.append(pl.BlockSpec((1, k), lambda i: (0, 0)))
    for o in outs:
        w = o["w"]
        n = w.shape[1]
        rope, post = o.get("rope"), o.get("post")
        plan.append(dict(rope=rope is not None, post=post is not None, scale=float(o.get("scale", 1.0))))
        args.append(w)
        in_specs.append(pl.BlockSpec((k, n), lambda i: (0, 0)))
        if rope is not None:
            w_sw, cos, sin = rope
            period = cos.shape[0] // tm
            assert cos.shape[0] % tm == 0
            args += [w_sw, cos, sin]
            in_specs.append(pl.BlockSpec((k, n), lambda i: (0, 0)))
            in_specs.append(pl.BlockSpec((tm, n), lambda i, period=period: (i % period, 0)))
            in_specs.append(pl.BlockSpec((tm, n), lambda i, period=period: (i % period, 0)))
        if post is not None:
            args.append(post.reshape(1, n))
            in_specs.append(pl.BlockSpec((1, n), lambda i: (0, 0)))
        out_shapes.append(jax.ShapeDtypeStruct((m, n), o.get("dtype", F32)))
        out_specs.append(pl.BlockSpec((tm, n), lambda i: (i, 0)))
    return pl.pallas_call(
        functools.partial(_norm_proj_kernel, plan),
        grid=(m // tm,),
        in_specs=in_specs,
        out_specs=out_specs,
        out_shape=out_shapes,
        compiler_params=_cparams("parallel"),
        name="norm_proj",
    )(*args)


def _proj_norm_res_kernel(h_ref, w_ref, g_ref, x_ref, o_ref):
    y = _dot(h_ref[...].astype(BF16), w_ref[...])
    o_ref[...] = x_ref[...] + _rms(y, g_ref[...])


def proj_norm_res(h, w, g, x, tm_pref=512):
    m, k = h.shape
    d = w.shape[1]
    tm = _row_tile(m, tm_pref)
    return pl.pallas_call(
        _proj_norm_res_kernel,
        grid=(m // tm,),
        in_specs=[
            pl.BlockSpec((tm, k), lambda i: (i, 0)),
            pl.BlockSpec((k, d), lambda i: (0, 0)),
            pl.BlockSpec((1, d), lambda i: (0, 0)),
            pl.BlockSpec((tm, d), lambda i: (i, 0)),
        ],
        out_specs=pl.BlockSpec((tm, d), lambda i: (i, 0)),
        out_shape=jax.ShapeDtypeStruct((m, d), F32),
        compiler_params=_cparams("parallel"),
        name="proj_norm_res",
    )(h, w, g.reshape(1, d), x)


def _gelu_tanh(x):
    c = 0.7978845608028654
    return 0.5 * x * (1.0 + jnp.tanh(c * (x + 0.044715 * (x * x * x))))


def _ffn_kernel(long_seq, tiles_per_seq, nf, x_ref, g5_ref, wg_ref, wu_ref, cw_ref, cb_ref, wd_ref, g6_ref,
                *refs):
    if long_seq:
        state_ref, o_ref, tail_ref, xn_ref, acc_ref, carry_ref = refs
    else:
        p1_ref, p2_ref, o_ref, gfull_ref, xn_ref, acc_ref = refs
    i = pl.program_id(0)
    f = pl.program_id(1)

    @pl.when(f == 0)
    def _():
        xn_ref[...] = _rms(x_ref[...], g5_ref[...]).astype(BF16)
        acc_ref[...] = jnp.zeros_like(acc_ref)

    xn = xn_ref[...]
    g = _dot(xn, wg_ref[...])
    u = _dot(xn, wu_ref[...])
    tm, fc = g.shape
    rows = lax.broadcasted_iota(jnp.int32, (tm, fc), 0)
    r1 = pltpu.roll(g, 1, 0)
    r2 = pltpu.roll(g, 2, 0)
    if long_seq:
        @pl.when(i % tiles_per_seq == 0)
        def _():
            carry_ref[f, 6:8, :] = state_ref[0]

        c0 = carry_ref[f, 6:7, :]
        c1 = carry_ref[f, 7:8, :]
        prev1 = jnp.where(rows == 0, c1, r1)
        prev2 = jnp.where(rows == 0, c0, jnp.where(rows == 1, c1, r2))
        carry_ref[f] = g[tm - 8:, :]
        tail_ref[0] = g[tm - 8:, :]
    else:
        pos = rows % 8
        prev1 = jnp.where(pos == 0, p1_ref[...], r1)
        prev2 = jnp.where(pos <= 1, p2_ref[...], r2)
        gfull_ref[...] = g
    gc = cb_ref[...] + cw_ref[0:1, :] * prev2 + cw_ref[1:2, :] * prev1 + cw_ref[2:3, :] * g
    act = (_gelu_tanh(gc) * u).astype(BF16)
    acc_ref[...] += _dot(act, wd_ref[...])

    @pl.when(f == nf - 1)
    def _():
        o_ref[...] = x_ref[...] + _rms(acc_ref[...], g6_ref[...])


def conv_ffn(x, g5, g6, wg, wu, cw, cb, wd, state, seq_len, tm_pref=1024):
    m, d = x.shape
    ff = wg.shape[1]
    fc = FFN_CHUNK
    nf = ff // fc
    nseq = m // seq_len
    long_seq = seq_len % 8 == 0 and seq_len > 8
    common_specs = [
        None,
        pl.BlockSpec((1, d), lambda i, f: (0, 0)),
        pl.BlockSpec((d, fc), lambda i, f: (0, f)),
        pl.BlockSpec((d, fc), lambda i, f: (0, f)),
        pl.BlockSpec((3, fc), lambda i, f: (0, f)),
        pl.BlockSpec((1, fc), lambda i, f: (0, f)),
        pl.BlockSpec((fc, d), lambda i, f: (f, 0)),
        pl.BlockSpec((1, d), lambda i, f: (0, 0)),
    ]
    if long_seq:
        tm = _row_tile(seq_len, tm_pref)
        tps = seq_len // tm
        nm = m // tm
        common_specs[0] = pl.BlockSpec((tm, d), lambda i, f: (i, 0))
        in_specs = common_specs + [pl.BlockSpec((1, 2, fc), lambda i, f: (i // tps, 0, f))]
        out_specs = [pl.BlockSpec((tm, d), lambda i, f: (i, 0)),
                     pl.BlockSpec((1, 8, fc), lambda i, f: (i, 0, f))]
        out_shape = [jax.ShapeDtypeStruct((m, d), F32), jax.ShapeDtypeStruct((nm, 8, ff), F32)]
        scratch = [pltpu.VMEM((tm, d), BF16), pltpu.VMEM((tm, d), F32), pltpu.VMEM((nf, 8, fc), F32)]
        kern = functools.partial(_ffn_kernel, True, tps, nf)
        args = (x, g5.reshape(1, d), wg, wu, cw, cb.reshape(1, ff), wd, g6.reshape(1, d), state)
        y, tails = pl.pallas_call(
            kern, grid=(nm, nf), in_specs=in_specs, out_specs=out_specs, out_shape=out_shape,
            scratch_shapes=scratch, compiler_params=_cparams("arbitrary", "arbitrary"), name="conv_ffn_long",
        )(*args)
        new_state = tails.reshape(nseq, tps, 8, ff)[:, -1, 6:8, :]
        return y, new_state
    assert seq_len == 8
    tm = _row_tile(m, tm_pref)
    nm = m // tm
    zeros = jnp.zeros((nseq, 8, ff), F32)
    p1 = zeros.at[:, 0].set(state[:, 1]).reshape(m, ff)
    p2 = zeros.at[:, 0].set(state[:, 0]).at[:, 1].set(state[:, 1]).reshape(m, ff)
    common_specs[0] = pl.BlockSpec((tm, d), lambda i, f: (i, 0))
    in_specs = common_specs + [pl.BlockSpec((tm, fc), lambda i, f: (i, f)), pl.BlockSpec((tm, fc), lambda i, f: (i, f))]
    out_specs = [pl.BlockSpec((tm, d), lambda i, f: (i, 0)), pl.BlockSpec((tm, fc), lambda i, f: (i, f))]
    out_shape = [jax.ShapeDtypeStruct((m, d), F32), jax.ShapeDtypeStruct((m, ff), F32)]
    scratch = [pltpu.VMEM((tm, d), BF16), pltpu.VMEM((tm, d), F32)]
    kern = functools.partial(_ffn_kernel, False, 1, nf)
    y, gfull = pl.pallas_call(
        kern, grid=(nm, nf), in_specs=in_specs, out_specs=out_specs, out_shape=out_shape,
        scratch_shapes=scratch, compiler_params=_cparams("parallel", "arbitrary"), name="conv_ffn_short",
    )(x, g5.reshape(1, d), wg, wu, cw, cb.reshape(1, ff), wd, g6.reshape(1, d), p1, p2)
    new_state = gfull.reshape(nseq, 8, ff)[:, 6:8, :]
    return y, new_state


def swap_halves(w, width):
    k, n = w.shape
    half = width // 2
    w3 = w.reshape(k, n // width, width)
    return jnp.concatenate([w3[..., half:], w3[..., :half]], axis=-1).reshape(k, n)


def rope_tables(pos, n_heads):
    half = MLA_ROPE // 2
    inv = ROPE_THETA ** (-jnp.arange(half, dtype=F32) / half)
    ang = pos.astype(F32)[:, None] * inv[None, :]
    cos, sin = jnp.cos(ang), jnp.sin(ang)
    cos2 = jnp.concatenate([cos, cos], axis=-1)
    sin2 = jnp.concatenate([-sin, sin], axis=-1)
    return jnp.tile(cos2, (1, n_heads)), jnp.tile(sin2, (1, n_heads))


LANES = 128


def mem_tile_rows(c):
    nl, nb, n, nh, hd = c.shape
    tiles = hd // LANES
    c = c.reshape(nl, nb, n, nh, tiles, LANES)
    return jnp.transpose(c, (0, 1, 2, 4, 3, 5)).reshape(nl, nb, n * tiles * nh, LANES)


def _xattn_kernel(tile_rows, n_mem, seqs, q_ref, k_ref, v_ref, o_ref):
    hd = XA_HEAD_DIM
    tiles = hd // LANES
    group = tiles * XA_HEADS
    sq = q_ref.shape[0] // seqs

    def head_slab(ref, j, h):
        if not tile_rows:
            return ref[j, :, h * hd:(h + 1) * hd]
        return jnp.concatenate(
            [ref[0, j, pl.ds(t * XA_HEADS + h, n_mem, stride=group), :] for t in range(tiles)], axis=-1)

    for j in range(seqs):
        rows = slice(j * sq, (j + 1) * sq)
        for h in range(XA_HEADS):
            sl = slice(h * hd, (h + 1) * hd)
            qh = q_ref[rows, sl].astype(BF16)
            kh = head_slab(k_ref, j, h).astype(BF16)
            vh = head_slab(v_ref, j, h).astype(BF16)
            s = _dot_nt(qh, kh)
            p = jnp.exp(s - jnp.max(s, axis=-1, keepdims=True))
            l = jnp.sum(p, axis=-1, keepdims=True)
            o_ref[rows, sl] = (_dot(p.astype(BF16), vh) / l).astype(o_ref.dtype)


def xattn_core(q, k, v, layer, seq_len, tq_pref=512, seqs_per_step=4):
    m, d = q.shape
    tq = _row_tile(seq_len, tq_pref)
    tpb = seq_len // tq
    seqs = seqs_per_step if (tpb == 1 and (m // seq_len) % seqs_per_step == 0) else 1
    tile_rows = k.ndim == 4
    if tile_rows:
        rows = k.shape[2]
        n_mem = rows * LANES // d
        kv_spec = pl.BlockSpec((1, seqs, rows, LANES), lambda i: (layer, i // tpb, 0, 0))
    else:
        n_mem = k.shape[1]
        kv_spec = pl.BlockSpec((seqs, n_mem, d), lambda i: (i // tpb, 0, 0))
    tq = tq * seqs
    return pl.pallas_call(
        functools.partial(_xattn_kernel, tile_rows, n_mem, seqs),
        grid=(m // tq,),
        in_specs=[pl.BlockSpec((tq, d), lambda i: (i, 0)), kv_spec, kv_spec],
        out_specs=pl.BlockSpec((tq, d), lambda i: (i, 0)),
        out_shape=jax.ShapeDtypeStruct((m, d), q.dtype),
        compiler_params=_cparams("parallel"),
        name="xattn_core",
    )(q, k, v)


def _sb_tri():
    j = lax.broadcasted_iota(jnp.int32, (2 * PAGE, 2 * PAGE), 0) % PAGE
    s = lax.broadcasted_iota(jnp.int32, (2 * PAGE, 2 * PAGE), 1)
    return jnp.where((s >= PAGE) | (j > s), 1.0, 0.0).astype(BF16)


def _sb_block(z, valid, r, tri):
    lsz = jnp.minimum(z, 0.0) - jnp.log(1.0 + jnp.exp(-jnp.abs(z)))
    ls = lsz - z
    if valid is not None:
        ls = jnp.where(valid, ls, 0.0)
    hi = ls.astype(BF16)
    lo = (ls - hi.astype(F32)).astype(BF16)
    cum = _dot(jnp.concatenate([hi, lo], axis=-1), tri)
    a = jnp.exp(lsz + cum[:, :PAGE] + r)
    if valid is not None:
        a = jnp.where(valid, a, 0.0)
    return a.astype(BF16), r + cum[:, PAGE:]


def _sb_prompt_kernel(q_ref, k_ref, v_ref, o_ref, kh_ref, vh_ref, qs_ref, r_ref, acc_ref):
    qi = pl.program_id(1)
    hd, grp, nkv = SB_HEAD_DIM, SB_GROUP, SB_KV_HEADS
    tq = q_ref.shape[0]

    @pl.when(qi == 0)
    def _():
        for h in range(nkv):
            kh_ref[h] = k_ref[:, h * hd:(h + 1) * hd].astype(BF16)
            vh_ref[h] = v_ref[:, h * hd:(h + 1) * hd].astype(BF16)

    for h in range(nkv):
        for g in range(grp):
            c = (h * grp + g) * hd
            qs_ref[h, g * tq:(g + 1) * tq, :] = q_ref[:, c:c + hd].astype(BF16)
    r_ref[...] = jnp.zeros_like(r_ref)
    acc_ref[...] = jnp.zeros_like(acc_ref)
    tri = _sb_tri()
    rows_t = lax.broadcasted_iota(jnp.int32, (grp * tq, PAGE), 0) % tq
    cols = lax.broadcasted_iota(jnp.int32, (grp * tq, PAGE), 1)

    def sweep(kb, valid):
        start = pl.multiple_of(kb * PAGE, PAGE)
        rmax = jnp.float32(-jnp.inf)
        for h in range(nkv):
            a, rn = _sb_block(_dot_nt(qs_ref[h], kh_ref[h, pl.ds(start, PAGE), :]), valid, r_ref[h], tri)
            acc_ref[h] += _dot(a, vh_ref[h, pl.ds(start, PAGE), :])
            r_ref[h] = rn
            rmax = jnp.maximum(rmax, jnp.max(rn))
        return rmax

    def cond(c):
        j, rmax = c
        return jnp.logical_and(j <= qi, rmax > -SB_EXIT)

    def body(c):
        j, _ = c
        return j + 1, sweep(qi - j, None)

    lax.while_loop(cond, body, (jnp.int32(1), sweep(qi, cols < rows_t)))
    o_ref[...] = jnp.concatenate(
        [acc_ref[h, g * tq:(g + 1) * tq, :] for h in range(nkv) for g in range(grp)], axis=-1).astype(o_ref.dtype)


def sb_attention_prompt(q, k, v, seq_len):
    m = q.shape[0]
    tq = PAGE
    nq = seq_len // tq
    nb = m // seq_len
    nkv, hd, grp = SB_KV_HEADS, SB_HEAD_DIM, SB_GROUP
    return pl.pallas_call(
        _sb_prompt_kernel,
        grid=(nb, nq),
        in_specs=[
            pl.BlockSpec((tq, SB_HEADS * hd), lambda b, i: (b * nq + i, 0)),
            pl.BlockSpec((seq_len, nkv * hd), lambda b, i: (b, 0)),
            pl.BlockSpec((seq_len, nkv * hd), lambda b, i: (b, 0)),
        ],
        out_specs=pl.BlockSpec((tq, SB_HEADS * hd), lambda b, i: (b * nq + i, 0)),
        out_shape=jax.ShapeDtypeStruct((m, SB_HEADS * hd), BF16),
        scratch_shapes=[
            pltpu.VMEM((nkv, seq_len, hd), BF16),
            pltpu.VMEM((nkv, seq_len, hd), BF16),
            pltpu.VMEM((nkv, grp * tq, hd), BF16),
            pltpu.VMEM((nkv, grp * tq, PAGE), F32),
            pltpu.VMEM((nkv, grp * tq, hd), F32),
        ],
        compiler_params=_cparams("parallel", "arbitrary"),
        name="sb_attention_prompt",
    )(q, k, v)


def _sb_sample_kernel(layer, n_pages, pt_ref, q_ref, kn_ref, vn_ref, kpool, vpool, o_ref,
                      kbuf, vbuf, sem, kfb, vfb, semfb, qs_ref, r_ref, acc_ref, knp_ref, vnp_ref):
    b = pl.program_id(0)
    nb = pl.num_programs(0)
    hd, grp, nkv = SB_HEAD_DIM, SB_GROUP, SB_KV_HEADS
    sq = q_ref.shape[0]
    last = n_pages - 1

    def newest_page(bb, slot):
        page = pt_ref[bb, last]
        return (pltpu.make_async_copy(kpool.at[layer, page], kbuf.at[slot], sem.at[slot, 0]),
                pltpu.make_async_copy(vpool.at[layer, page], vbuf.at[slot], sem.at[slot, 1]))

    @pl.when(b == 0)
    def _():
        for c in newest_page(0, 0):
            c.start()

    @pl.when(b + 1 < nb)
    def _():
        for c in newest_page(b + 1, (b + 1) % 2):
            c.start()

    for h in range(nkv):
        for g in range(grp):
            c = (h * grp + g) * hd
            qs_ref[h, g * sq:(g + 1) * sq, :] = q_ref[:, c:c + hd]
    r_ref[...] = jnp.zeros_like(r_ref)
    acc_ref[...] = jnp.zeros_like(acc_ref)
    knp_ref[...] = jnp.zeros_like(knp_ref)
    vnp_ref[...] = jnp.zeros_like(vnp_ref)
    knp_ref[0:sq, :] = kn_ref[...]
    vnp_ref[0:sq, :] = vn_ref[...]
    tri = _sb_tri()
    rows_t = lax.broadcasted_iota(jnp.int32, (grp * sq, PAGE), 0) % sq
    cols = lax.broadcasted_iota(jnp.int32, (grp * sq, PAGE), 1)

    def sweep(logits, weighted_values, valid):
        rmax = jnp.float32(-jnp.inf)
        for h in range(nkv):
            a, rn = _sb_block(logits(h, qs_ref[h].astype(BF16)), valid, r_ref[h], tri)
            acc_ref[h] += weighted_values(h, a)
            r_ref[h] = rn
            rmax = jnp.maximum(rmax, jnp.max(rn))
        return rmax

    def sweep_page(k_ref, v_ref):
        return sweep(lambda h, qs: _dot(qs, k_ref[h].astype(BF16)),
                     lambda h, a: _dot_nt(a, v_ref[h].astype(BF16)), None)

    sweep(lambda h, qs: _dot_nt(qs, knp_ref[:, h * hd:(h + 1) * hd].astype(BF16)),
          lambda h, a: _dot(a, vnp_ref[:, h * hd:(h + 1) * hd].astype(BF16)), cols < rows_t)
    slot = b % 2
    for c in newest_page(b, slot):
        c.wait()
    rmax = sweep_page(kbuf.at[slot], vbuf.at[slot])

    def cond(c):
        p, rm = c
        return jnp.logical_and(p >= 0, rm > -SB_EXIT)

    def body(c):
        p, _ = c
        page = pt_ref[b, p]
        ck = pltpu.make_async_copy(kpool.at[layer, page], kfb, semfb.at[0])
        cv = pltpu.make_async_copy(vpool.at[layer, page], vfb, semfb.at[1])
        ck.start()
        cv.start()
        ck.wait()
        cv.wait()
        return p - 1, sweep_page(kfb, vfb)

    lax.while_loop(cond, body, (jnp.int32(last - 1), rmax))
    o_ref[...] = jnp.concatenate(
        [acc_ref[h, g * sq:(g + 1) * sq, :] for h in range(nkv) for g in range(grp)], axis=-1).astype(o_ref.dtype)


def sb_attention_sample(q, k_new, v_new, k_pool, v_pool, layer, page_table, seq_len):
    m = q.shape[0]
    nb, n_pages = page_table.shape
    assert m == nb * seq_len and seq_len == 8
    nkv, hd, grp = SB_KV_HEADS, SB_HEAD_DIM, SB_GROUP
    grid_spec = pltpu.PrefetchScalarGridSpec(
        num_scalar_prefetch=1,
        grid=(nb,),
        in_specs=[
            pl.BlockSpec((seq_len, SB_HEADS * hd), lambda b, pt: (b, 0)),
            pl.BlockSpec((seq_len, nkv * hd), lambda b, pt: (b, 0)),
            pl.BlockSpec((seq_len, nkv * hd), lambda b, pt: (b, 0)),
            pl.BlockSpec(memory_space=pl.ANY),
            pl.BlockSpec(memory_space=pl.ANY),
        ],
        out_specs=pl.BlockSpec((seq_len, SB_HEADS * hd), lambda b, pt: (b, 0)),
        scratch_shapes=[
            pltpu.VMEM((2, nkv, hd, PAGE), F32),
            pltpu.VMEM((2, nkv, hd, PAGE), F32),
            pltpu.SemaphoreType.DMA((2, 2)),
            pltpu.VMEM((nkv, hd, PAGE), F32),
            pltpu.VMEM((nkv, hd, PAGE), F32),
            pltpu.SemaphoreType.DMA((2,)),
            pltpu.VMEM((nkv, grp * seq_len, hd), F32),
            pltpu.VMEM((nkv, grp * seq_len, PAGE), F32),
            pltpu.VMEM((nkv, grp * seq_len, hd), F32),
            pltpu.VMEM((PAGE, nkv * hd), F32),
            pltpu.VMEM((PAGE, nkv * hd), F32),
        ],
    )
    return pl.pallas_call(
        functools.partial(_sb_sample_kernel, layer, n_pages),
        grid_spec=grid_spec,
        out_shape=jax.ShapeDtypeStruct((m, SB_HEADS * hd), F32),
        compiler_params=_cparams("arbitrary"),
        name="sb_attention_sample",
    )(page_table, q, k_new, v_new, k_pool, v_pool)


def _mla_load_queries(qn_ref, qr_ref, wuk_ref, ql_ref, qr_s_ref):
    tq = qn_ref.shape[0]
    for h in range(MLA_HEADS):
        qn = qn_ref[:, h * MLA_NOPE:(h + 1) * MLA_NOPE].astype(BF16)
        ql_ref[h * tq:(h + 1) * tq, :] = (_dot_nt(qn, wuk_ref[h]) * MLA_SCALE).astype(ql_ref.dtype)
        qr_s_ref[h * tq:(h + 1) * tq, :] = (qr_ref[:, h * MLA_ROPE:(h + 1) * MLA_ROPE] * MLA_SCALE).astype(qr_s_ref.dtype)


def _lane_tile(x, width):
    return jnp.concatenate([x] * (width // LANES), axis=-1)


def _mla_step(s, ck, valid, m_ref, l_ref, acc_ref, rows):
    if valid is not None:
        s = jnp.where(valid, s, NEG_BIG)
    m_prev = m_ref[rows, :]
    m_new = jnp.maximum(m_prev, jnp.max(s, axis=-1, keepdims=True))
    alpha = jnp.exp(m_prev - m_new)
    p = jnp.exp(s - _lane_tile(m_new, s.shape[1]))
    l_ref[rows, :] = alpha * l_ref[rows, :] + jnp.sum(p, axis=-1, keepdims=True)
    acc_ref[rows, :] = _lane_tile(alpha, acc_ref.shape[1]) * acc_ref[rows, :] + _dot(p.astype(BF16), ck)
    m_ref[rows, :] = m_new


def _mla_store_output(wuv_ref, l_ref, acc_ref, o_ref):
    tq = o_ref.shape[0]
    for h in range(MLA_HEADS):
        rows = slice(h * tq, (h + 1) * tq)
        ol = (acc_ref[rows, :] / _lane_tile(l_ref[rows, :], acc_ref.shape[1])).astype(BF16)
        o_ref[:, h * MLA_VDIM:(h + 1) * MLA_VDIM] = _dot(ol, wuv_ref[h]).astype(o_ref.dtype)


def _mla_prompt_kernel(tk, qn_ref, qr_ref, ckv_ref, kr_ref, wuk_ref, wuv_ref, o_ref,
                       ck_s, kr_s, ql_s, qr_s, m_s, l_s, acc_s):
    qi = pl.program_id(1)
    tq = qn_ref.shape[0]
    rows_n = MLA_HEADS * tq

    @pl.when(qi == 0)
    def _():
        ck_s[...] = ckv_ref[...].astype(BF16)
        kr_s[...] = kr_ref[...].astype(BF16)

    _mla_load_queries(qn_ref, qr_ref, wuk_ref, ql_s, qr_s)
    m_s[...] = jnp.full_like(m_s, NEG_BIG)
    l_s[...] = jnp.zeros_like(l_s)
    acc_s[...] = jnp.zeros_like(acc_s)
    grp_rows = MLA_GROUP_HEADS * tq
    q_pos = qi * tq + lax.broadcasted_iota(jnp.int32, (grp_rows, tk), 0) % tq
    cols = lax.broadcasted_iota(jnp.int32, (grp_rows, tk), 1)

    def block(kb, masked):
        start = pl.multiple_of(kb * tk, tk)
        ck = ck_s[pl.ds(start, tk), :]
        kr = kr_s[pl.ds(start, tk), :]
        valid = (cols + kb * tk <= q_pos) if masked else None
        for r0 in range(0, rows_n, grp_rows):
            rows = slice(r0, r0 + grp_rows)
            s = _dot_nt(ql_s[rows, :], ck) + _dot_nt(qr_s[rows, :], kr)
            _mla_step(s, ck, valid, m_s, l_s, acc_s, rows)

    def body(kb, carry):
        block(kb, False)
        return carry

    n_blocks = ((qi + 1) * tq + tk - 1) // tk
    lax.fori_loop(0, n_blocks - 1, body, 0)
    block(n_blocks - 1, True)
    _mla_store_output(wuv_ref, l_s, acc_s, o_ref)


def mla_attention_prompt(qn, qr, ckv, kr, wuk, wuv, seq_len):
    m = qn.shape[0]
    tq = PAGE
    tk = 2 * PAGE if seq_len % (2 * PAGE) == 0 else PAGE
    nq = seq_len // tq
    nb = m // seq_len
    rows_n = MLA_HEADS * tq
    lat = MLA_KV_LORA
    return pl.pallas_call(
        functools.partial(_mla_prompt_kernel, tk),
        grid=(nb, nq),
        in_specs=[
            pl.BlockSpec((tq, MLA_HEADS * MLA_NOPE), lambda b, i: (b * nq + i, 0)),
            pl.BlockSpec((tq, MLA_HEADS * MLA_ROPE), lambda b, i: (b * nq + i, 0)),
            pl.BlockSpec((seq_len, lat), lambda b, i: (b, 0)),
            pl.BlockSpec((seq_len, MLA_ROPE), lambda b, i: (b, 0)),
            pl.BlockSpec((MLA_HEADS, lat, MLA_NOPE), lambda b, i: (0, 0, 0)),
            pl.BlockSpec((MLA_HEADS, lat, MLA_VDIM), lambda b, i: (0, 0, 0)),
        ],
        out_specs=pl.BlockSpec((tq, MLA_HEADS * MLA_VDIM), lambda b, i: (b * nq + i, 0)),
        out_shape=jax.ShapeDtypeStruct((m, MLA_HEADS * MLA_VDIM), BF16),
        scratch_shapes=[
            pltpu.VMEM((seq_len, lat), BF16),
            pltpu.VMEM((seq_len, MLA_ROPE), BF16),
            pltpu.VMEM((rows_n, lat), BF16),
            pltpu.VMEM((rows_n, MLA_ROPE), BF16),
            pltpu.VMEM((rows_n, LANES), F32),
            pltpu.VMEM((rows_n, LANES), F32),
            pltpu.VMEM((rows_n, lat), F32),
        ],
        compiler_params=_cparams("parallel", "arbitrary"),
        name="mla_attention_prompt",
    )(qn, qr, ckv, kr, wuk, wuv)


def _mla_sample_kernel(layer, n_pages, ppi, pt_ref, qn_ref, qr_ref, cn_ref, rn_ref, cpool, rpool, wuk_ref, wuv_ref, o_ref,
                       cbuf, rbuf, sem, cnp_s, rnp_s, ql_s, qr_s, ck_all, s_all):
    b = pl.program_id(0)
    nb = pl.num_programs(0)
    sq = qn_ref.shape[0]
    rows_n = MLA_HEADS * sq
    n_it = n_pages // ppi

    def group_copies(bb, it, slot):
        out = []
        for j in range(ppi):
            page = pt_ref[bb, it * ppi + j]
            out.append(pltpu.make_async_copy(cpool.at[layer, page], cbuf.at[slot, pl.ds(j * PAGE, PAGE), :],
                                             sem.at[slot, 0, j]))
            out.append(pltpu.make_async_copy(rpool.at[layer, page], rbuf.at[slot, j], sem.at[slot, 1, j]))
        return out

    n_slots = cbuf.shape[0]
    total = nb * n_it

    def start_group(g):
        for c in group_copies(g // n_it, g % n_it, g % n_slots):
            c.start()

    @pl.when(b == 0)
    def _():
        for g in range(n_slots - 1):
            if g < total:
                start_group(g)

    _mla_load_queries(qn_ref, qr_ref, wuk_ref, ql_s, qr_s)
    cnp_s[...] = jnp.zeros_like(cnp_s)
    rnp_s[...] = jnp.zeros_like(rnp_s)
    cnp_s[0:sq, :] = cn_ref[...]
    rnp_s[0:sq, :] = rn_ref[...]
    rows_t = lax.broadcasted_iota(jnp.int32, (rows_n, PAGE), 0) % sq
    cols = lax.broadcasted_iota(jnp.int32, (rows_n, PAGE), 1)
    ql = ql_s[...].astype(BF16)
    qr = qr_s[...].astype(BF16)
    cn = cnp_s[...].astype(BF16)
    s_new = jnp.where(cols <= rows_t, _dot_nt(ql, cn) + _dot_nt(qr, rnp_s[...].astype(BF16)), NEG_BIG)

    def body(it, carry):
        g = b * n_it + it
        slot = g % n_slots

        @pl.when(g + n_slots - 1 < total)
        def _():
            start_group(g + n_slots - 1)

        for c in group_copies(b, it, slot):
            c.wait()
        ck = cbuf[slot].astype(BF16)
        ck_all[it] = ck
        s_rope = jnp.concatenate([_dot(qr, rbuf[slot, j].astype(BF16)) for j in range(ppi)], axis=-1)
        s_all[it] = _dot_nt(ql, ck) + s_rope
        return carry

    lax.fori_loop(0, n_it, body, 0)

    m_run = jnp.max(s_new, axis=-1, keepdims=True)
    for it in range(n_it):
        m_run = jnp.maximum(m_run, jnp.max(s_all[it], axis=-1, keepdims=True))
    p_new = jnp.exp(s_new - m_run)
    l_run = jnp.sum(p_new, axis=-1, keepdims=True)
    acc = _dot(p_new.astype(BF16), cn)
    for it in range(n_it):
        p = jnp.exp(s_all[it] - m_run)
        l_run = l_run + jnp.sum(p, axis=-1, keepdims=True)
        acc = acc + _dot(p.astype(BF16), ck_all[it])
    acc = acc / l_run
    for h in range(MLA_HEADS):
        ol = acc[h * sq:(h + 1) * sq, :].astype(BF16)
        o_ref[:, h * MLA_VDIM:(h + 1) * MLA_VDIM] = _dot(ol, wuv_ref[h]).astype(o_ref.dtype)


def mla_attention_sample(qn, qr, c_new, r_new, c_pool, r_pool, layer, page_table, wuk, wuv, seq_len):
    m = qn.shape[0]
    nb, n_pages = page_table.shape
    assert m == nb * seq_len and seq_len == 8
    ppi = MLA_PAGES_PER_GROUP if n_pages % MLA_PAGES_PER_GROUP == 0 else 1
    n_it = n_pages // ppi
    n_slots = 3
    rows_n = MLA_HEADS * seq_len
    lat = MLA_KV_LORA
    grid_spec = pltpu.PrefetchScalarGridSpec(
        num_scalar_prefetch=1,
        grid=(nb,),
        in_specs=[
            pl.BlockSpec((seq_len, MLA_HEADS * MLA_NOPE), lambda b, pt: (b, 0)),
            pl.BlockSpec((seq_len, MLA_HEADS * MLA_ROPE), lambda b, pt: (b, 0)),
            pl.BlockSpec((seq_len, lat), lambda b, pt: (b, 0)),
            pl.BlockSpec((seq_len, MLA_ROPE), lambda b, pt: (b, 0)),
            pl.BlockSpec(memory_space=pl.ANY),
            pl.BlockSpec(memory_space=pl.ANY),
            pl.BlockSpec((MLA_HEADS, lat, MLA_NOPE), lambda b, pt: (0, 0, 0)),
            pl.BlockSpec((MLA_HEADS, lat, MLA_VDIM), lambda b, pt: (0, 0, 0)),
        ],
        out_specs=pl.BlockSpec((seq_len, MLA_HEADS * MLA_VDIM), lambda b, pt: (b, 0)),
        scratch_shapes=[
            pltpu.VMEM((n_slots, ppi * PAGE, lat), F32),
            pltpu.VMEM((n_slots, ppi, MLA_ROPE, PAGE), F32),
            pltpu.SemaphoreType.DMA((n_slots, 2, ppi)),
            pltpu.VMEM((PAGE, lat), F32),
            pltpu.VMEM((PAGE, MLA_ROPE), F32),
            pltpu.VMEM((rows_n, lat), F32),
            pltpu.VMEM((rows_n, MLA_ROPE), F32),
            pltpu.VMEM((n_it, ppi * PAGE, lat), BF16),
            pltpu.VMEM((n_it, rows_n, ppi * PAGE), F32),
        ],
    )
    return pl.pallas_call(
        functools.partial(_mla_sample_kernel, layer, n_pages, ppi),
        grid_spec=grid_spec,
        out_shape=jax.ShapeDtypeStruct((m, MLA_HEADS * MLA_VDIM), F32),
        compiler_params=_cparams("arbitrary"),
        name="mla_attention_sample",
    )(page_table, qn, qr, c_new, r_new, c_pool, r_pool, wuk, wuv)


def _split3(x):
    a = x.astype(BF16)
    r = x - a.astype(F32)
    b = r.astype(BF16)
    c = (r - b.astype(F32)).astype(BF16)
    return a, b, c


def _mlstm_kernel(lr, nc, q_ref, k_ref, v_ref, og_ref, gt_ref, bias_ref, hg_ref, c0_ref, n0_ref, m0_ref,
                  out_ref, c_out, n_out, m_out, c_s, n_s, m_s, *pad):
    c_idx = pl.program_id(1)
    L = ML_CHUNK
    dk, dv, nh = ML_DK, ML_DV, ML_HEADS

    @pl.when(c_idx == 0)
    def _():
        c_s[...] = c0_ref[0]
        n_s[...] = n0_ref[0]
        m_s[...] = m0_ref[0]

    if lr == L:
        q_all, k_all, v_all, gates = q_ref[...], k_ref[...], v_ref[...], gt_ref[...]
    else:
        qp, kp, vp, gp = pad
        for dst, src in ((qp, q_ref), (kp, k_ref), (vp, v_ref), (gp, gt_ref)):
            dst[...] = jnp.zeros_like(dst)
            dst[0:lr, :] = src[...]
        q_all, k_all, v_all, gates = qp[...], kp[...], vp[...], gp[...]

    gb = gates + bias_ref[...]
    lf = jnp.minimum(gb, 0.0) - jnp.log(1.0 + jnp.exp(-jnp.abs(gb)))
    row = lax.broadcasted_iota(jnp.int32, (L, 128), 0)
    if lr != L:
        lf = jnp.where(row < lr, lf, 0.0)
        gb = jnp.where(row < lr, gb, NEG_BIG)
    li = lax.broadcasted_iota(jnp.int32, (L, L), 0)
    si = lax.broadcasted_iota(jnp.int32, (L, L), 1)
    causal = si <= li
    tril = jnp.where(causal, 1.0, 0.0).astype(BF16)
    t0, t1, t2 = _split3(lf)
    bcum = _dot(tril, t0) + _dot(tril, t1) + _dot(tril, t2)
    bcum_t = bcum.T
    gb_t = gb.T

    for h in range(nh):
        qh = q_all[:, h * dk:(h + 1) * dk]
        kh = k_all[:, h * dk:(h + 1) * dk] * (dk ** -0.5)
        vh = v_all[:, h * dv:(h + 1) * dv]
        qb, kb, vb = qh.astype(BF16), kh.astype(BF16), vh.astype(BF16)
        b_col = bcum[:, nh + h:nh + h + 1]
        i_col = gb[:, h:h + 1]
        b_row = bcum_t[nh + h:nh + h + 1, :]
        i_row = gb_t[h:h + 1, :]
        m_old = m_s[h:h + 1, 0:1]
        n_old = n_s[h:h + 1, :]
        c_old = c_s[h]

        d = jnp.where(causal, b_col - b_row + i_row, NEG_BIG)
        inter = b_col + m_old
        m_q = jnp.maximum(inter, jnp.max(d, axis=-1, keepdims=True))
        sc = _dot_nt(qb, kb) * jnp.exp(d - m_q)
        w_st = jnp.exp(inter - m_q)
        num = _dot(sc.astype(BF16), vb) + w_st * _dot_nt(qb, c_old.astype(BF16))
        den = jnp.sum(sc, axis=-1, keepdims=True) + w_st * jnp.sum(qh * n_old, axis=-1, keepdims=True)
        hc = num / jnp.maximum(jnp.abs(den), jnp.exp(-m_q))

        b_end = bcum[L - 1:L, nh + h:nh + h + 1]
        g_col = b_end - b_col + i_col
        g_row = b_end - b_row + i_row
        m_new = jnp.maximum(b_end + m_old, jnp.max(g_row, axis=-1, keepdims=True))
        w_end = jnp.exp(g_col - m_new)
        decay = jnp.exp(b_end + m_old - m_new)
        c_s[h] = decay * c_old + _dot_tn((vh * w_end).astype(BF16), kb)
        n_s[h:h + 1, :] = decay * n_old + jnp.sum(w_end * kh, axis=0, keepdims=True)
        m_s[h:h + 1, :] = jnp.broadcast_to(m_new, (1, 128))

        hn = _rms(hc, hg_ref[:, h * dv:(h + 1) * dv])
        og = og_ref[...][:, h * dv:(h + 1) * dv]
        gate = 1.0 / (1.0 + jnp.exp(-og))
        out_ref[:, h * dv:(h + 1) * dv] = (gate * hn[0:lr, :]).astype(out_ref.dtype)

    @pl.when(c_idx == nc - 1)
    def _():
        c_out[0] = c_s[...]
        n_out[0] = n_s[...]
        m_out[0] = m_s[...]


def mlstm_core(q, k, v, og, gates, bias, head_g, c0, n0, m0, seq_len, out_dtype):
    m_rows = q.shape[0]
    nb = m_rows // seq_len
    lr = min(ML_CHUNK, seq_len)
    nc = seq_len // lr
    nh, dk, dv = ML_HEADS, ML_DK, ML_DV
    n0p = jnp.zeros((nb, 8, dk), F32).at[:, :nh].set(n0)
    m0p = jnp.zeros((nb, 8, 128), F32).at[:, :nh].set(jnp.broadcast_to(m0[:, :, None], (nb, nh, 128)))
    row_spec = lambda w: pl.BlockSpec((lr, w), lambda b, c: (b * nc + c, 0))
    scratch = [pltpu.VMEM((nh, dv, dk), F32), pltpu.VMEM((8, dk), F32), pltpu.VMEM((8, 128), F32)]
    if lr != ML_CHUNK:
        scratch += [pltpu.VMEM((ML_CHUNK, nh * dk), F32), pltpu.VMEM((ML_CHUNK, nh * dk), F32),
                    pltpu.VMEM((ML_CHUNK, nh * dv), F32), pltpu.VMEM((ML_CHUNK, 128), F32)]
    out, c_f, n_f, m_f = pl.pallas_call(
        functools.partial(_mlstm_kernel, lr, nc),
        grid=(nb, nc),
        in_specs=[
            row_spec(nh * dk), row_spec(nh * dk), row_spec(nh * dv), row_spec(nh * dv), row_spec(128),
            pl.BlockSpec((1, 128), lambda b, c: (0, 0)),
            pl.BlockSpec((1, nh * dv), lambda b, c: (0, 0)),
            pl.BlockSpec((1, nh, dv, dk), lambda b, c: (b, 0, 0, 0)),
            pl.BlockSpec((1, 8, dk), lambda b, c: (b, 0, 0)),
            pl.BlockSpec((1, 8, 128), lambda b, c: (b, 0, 0)),
        ],
        out_specs=[
            row_spec(nh * dv),
            pl.BlockSpec((1, nh, dv, dk), lambda b, c: (b, 0, 0, 0)),
            pl.BlockSpec((1, 8, dk), lambda b, c: (b, 0, 0)),
            pl.BlockSpec((1, 8, 128), lambda b, c: (b, 0, 0)),
        ],
        out_shape=[
            jax.ShapeDtypeStruct((m_rows, nh * dv), out_dtype),
            jax.ShapeDtypeStruct((nb, nh, dv, dk), F32),
            jax.ShapeDtypeStruct((nb, 8, dk), F32),
            jax.ShapeDtypeStruct((nb, 8, 128), F32),
        ],
        scratch_shapes=scratch,
        compiler_params=_cparams("parallel", "arbitrary"),
        name="mlstm_core",
    )(q, k, v, og, gates, bias.reshape(1, 128), head_g.reshape(1, nh * dv), c0, n0p, m0p)
    return out, c_f, n_f[:, :nh], m_f[:, :nh, 0]


def prep_sb(w_qkv, w_o):
    nq, nkv = SB_HEADS * SB_HEAD_DIM, SB_KV_HEADS * SB_HEAD_DIM
    return dict(wq=w_qkv[:, :nq].astype(BF16), wk=w_qkv[:, nq:nq + nkv].astype(BF16),
                wv=w_qkv[:, nq + nkv:].astype(BF16), wo=w_o.astype(BF16))


def sb_mixer(x, g, w, seq_len, past=None):
    q, k, v = norm_proj(x, g, [dict(w=w["wq"], scale=SB_SCALE), dict(w=w["wk"]), dict(w=w["wv"])])
    if past is None:
        o = sb_attention_prompt(q, k, v, seq_len)
    else:
        k_pool, v_pool, layer, page_table = past
        o = sb_attention_sample(q, k, v, k_pool, v_pool, layer, page_table, seq_len)
    return o, k, v


def prep_ml(w_in, b_gates, head_g, w_out):
    o0 = 2 * ML_QK + ML_HEADS * ML_DV
    gates_w = jnp.zeros((D_MODEL, 128), F32).at[:, :2 * ML_HEADS].set(w_in[:, o0 + D_MODEL:])
    bias = jnp.zeros((128,), F32).at[:2 * ML_HEADS].set(b_gates)
    return dict(wq=w_in[:, :ML_QK].astype(BF16), wk=w_in[:, ML_QK:2 * ML_QK].astype(BF16),
                wv=w_in[:, 2 * ML_QK:o0].astype(BF16), wog=w_in[:, o0:o0 + D_MODEL].astype(BF16),
                wgt=gates_w.astype(BF16), bias=bias, head_g=head_g.reshape(-1), wo=w_out.astype(BF16))


def ml_mixer(x, g, w, seq_len, state, out_dtype):
    q, k, v, og, gates = norm_proj(
        x, g, [dict(w=w["wq"]), dict(w=w["wk"]), dict(w=w["wv"]), dict(w=w["wog"]), dict(w=w["wgt"])])
    c0, n0, m0 = state
    return mlstm_core(q, k, v, og, gates, w["bias"], w["head_g"], c0, n0, m0, seq_len, out_dtype)


def prep_mla(w_dq, g_q, w_uq, w_dkv, g_kv, w_uk, w_uv, w_o):
    per = MLA_NOPE + MLA_ROPE
    w_uq3 = w_uq.reshape(w_uq.shape[0], MLA_HEADS, per)
    w_nope = w_uq3[:, :, :MLA_NOPE].reshape(-1, MLA_HEADS * MLA_NOPE)
    w_rope = w_uq3[:, :, MLA_NOPE:].reshape(-1, MLA_HEADS * MLA_ROPE)
    w_c, w_r = w_dkv[:, :MLA_KV_LORA], w_dkv[:, MLA_KV_LORA:]
    return dict(wdq=w_dq.astype(BF16), g_q=g_q, w_nope=w_nope.astype(BF16), w_rope=w_rope.astype(BF16),
                w_rope_sw=swap_halves(w_rope, MLA_ROPE).astype(BF16), w_c=w_c.astype(BF16), g_kv=g_kv,
                w_r=w_r.astype(BF16), w_r_sw=swap_halves(w_r, MLA_ROPE).astype(BF16),
                wuk=jnp.transpose(w_uk, (1, 0, 2)).astype(BF16), wuv=jnp.transpose(w_uv, (1, 0, 2)).astype(BF16),
                wo=w_o.astype(BF16))


def mla_mixer(x, g, w, seq_len, tables, past=None):
    cos1, sin1, cos8, sin8 = tables
    cq, ckv, kr = norm_proj(x, g, [
        dict(w=w["wdq"]),
        dict(w=w["w_c"], post=w["g_kv"]),
        dict(w=w["w_r"], rope=(w["w_r_sw"], cos1, sin1)),
    ], tm_pref=cos1.shape[0] if cos1.shape[0] < 512 else 512)
    qn, qr = norm_proj(cq, w["g_q"], [
        dict(w=w["w_nope"]),
        dict(w=w["w_rope"], rope=(w["w_rope_sw"], cos8, sin8)),
    ], tm_pref=cos1.shape[0] if cos1.shape[0] < 512 else 512)
    if past is None:
        o = mla_attention_prompt(qn, qr, ckv, kr, w["wuk"], w["wuv"], seq_len)
    else:
        c_pool, r_pool, layer, page_table = past
        o = mla_attention_sample(qn, qr, ckv, kr, c_pool, r_pool, layer, page_table, w["wuk"], w["wuv"], seq_len)
    return o, ckv, kr


def kernel(x_prompt, x_sample, cache_sb_k, cache_sb_v, cache_mla_ckv, cache_mla_kr, state_ml_C, state_ml_n, state_ml_m, state_ffn_conv, cache_mem_k, cache_mem_v, page_table, mem_prompt, norm_g, sb_w_qkv, sb_w_o, ml_w_in, ml_b_gates, ml_head_g, ml_w_out, mla_w_dq, mla_g_q, mla_w_uq, mla_w_dkv, mla_g_kv, mla_w_uk, mla_w_uv, mla_w_o, xa_w_q, xa_w_kv, xa_w_o, ffn_w_gate, ffn_w_up, ffn_conv_w, ffn_conv_b, ffn_w_down):
    bp, sp, d = x_prompt.shape
    bs, ss, _ = x_sample.shape
    depth = norm_g.shape[0]
    n_pages = page_table.shape[1]
    past_len = n_pages * cache_sb_k.shape[2]
    n_mem = mem_prompt.shape[1]
    ff = ffn_w_gate.shape[2]
    sb_k_pages = jnp.transpose(cache_sb_k, (0, 1, 3, 4, 2))
    sb_v_pages = jnp.transpose(cache_sb_v, (0, 1, 3, 4, 2))
    mla_kr_pages = jnp.transpose(cache_mla_kr, (0, 1, 3, 2))
    mem_k_rows = mem_tile_rows(cache_mem_k)
    mem_v_rows = mem_tile_rows(cache_mem_v)

    xp = x_prompt.reshape(bp * sp, d)
    xs = x_sample.reshape(bs * ss, d)
    mem = mem_prompt.reshape(bp * n_mem, d)

    pos_p = jnp.arange(sp, dtype=jnp.int32)
    rows_s = min(512, bs * ss)
    pos_s = past_len + jnp.arange(rows_s, dtype=jnp.int32) % ss
    tabs_p = rope_tables(pos_p, 1) + rope_tables(pos_p, MLA_HEADS)
    tabs_s = rope_tables(pos_s, 1) + rope_tables(pos_s, MLA_HEADS)

    outs = {name: [] for name in (
        "p_sb_k", "p_sb_v", "p_ckv", "p_kr", "p_C", "p_n", "p_m", "p_conv", "p_mk", "p_mv",
        "s_sb_k", "s_sb_v", "s_ckv", "s_kr", "s_C", "s_n", "s_m", "s_conv")}

    for i in range(depth):
        kind, j = i % N_MIXERS, i // N_MIXERS
        g = norm_g[i]
        if kind == 0:
            w = prep_sb(sb_w_qkv[j], sb_w_o[j])
            mp, kp, vp = sb_mixer(xp, g[0], w, sp)
            past = (sb_k_pages, sb_v_pages, j, page_table)
            ms, ks_, vs_ = sb_mixer(xs, g[0], w, ss, past)
            outs["p_sb_k"].append(kp.reshape(bp, sp, SB_KV_HEADS, SB_HEAD_DIM))
            outs["p_sb_v"].append(vp.reshape(bp, sp, SB_KV_HEADS, SB_HEAD_DIM))
            outs["s_sb_k"].append(ks_.reshape(bs, ss, SB_KV_HEADS, SB_HEAD_DIM))
            outs["s_sb_v"].append(vs_.reshape(bs, ss, SB_KV_HEADS, SB_HEAD_DIM))
        elif kind == 1:
            w = prep_ml(ml_w_in[j], ml_b_gates[j], ml_head_g[j], ml_w_out[j])
            zero_state = (jnp.zeros((bp, ML_HEADS, ML_DV, ML_DK), F32), jnp.zeros((bp, ML_HEADS, ML_DK), F32),
                          jnp.zeros((bp, ML_HEADS), F32))
            mp, c_p, n_p, m_p = ml_mixer(xp, g[0], w, sp, zero_state, BF16)
            ms, c_s, n_s, m_s = ml_mixer(xs, g[0], w, ss, (state_ml_C[j], state_ml_n[j], state_ml_m[j]), F32)
            outs["p_C"].append(c_p); outs["p_n"].append(n_p); outs["p_m"].append(m_p)
            outs["s_C"].append(c_s); outs["s_n"].append(n_s); outs["s_m"].append(m_s)
        else:
            w = prep_mla(mla_w_dq[j], mla_g_q[j], mla_w_uq[j], mla_w_dkv[j], mla_g_kv[j], mla_w_uk[j],
                         mla_w_uv[j], mla_w_o[j])
            mp, c_p, r_p = mla_mixer(xp, g[0], w, sp, tabs_p)
            ms, c_s, r_s = mla_mixer(xs, g[0], w, ss, tabs_s, past=(cache_mla_ckv, mla_kr_pages, j, page_table))
            outs["p_ckv"].append(c_p.reshape(bp, sp, -1)); outs["p_kr"].append(r_p.reshape(bp, sp, -1))
            outs["s_ckv"].append(c_s.reshape(bs, ss, -1)); outs["s_kr"].append(r_s.reshape(bs, ss, -1))
        xp = proj_norm_res(mp, w["wo"], g[1], xp)
        xs = proj_norm_res(ms, w["wo"], g[1], xs)

        w_kv = xa_w_kv[i]
        mk, mv = norm_proj(mem, g[2], [dict(w=w_kv[:, :d].astype(BF16)), dict(w=w_kv[:, d:].astype(BF16))])
        outs["p_mk"].append(mk.reshape(bp, n_mem, XA_HEADS, XA_HEAD_DIM))
        outs["p_mv"].append(mv.reshape(bp, n_mem, XA_HEADS, XA_HEAD_DIM))
        wq, wo = xa_w_q[i].astype(BF16), xa_w_o[i].astype(BF16)
        (qp,) = norm_proj(xp, g[3], [dict(w=wq, scale=XA_SCALE, dtype=BF16)])
        ap = xattn_core(qp, mk.reshape(bp, n_mem, d), mv.reshape(bp, n_mem, d), 0, sp)
        xp = proj_norm_res(ap, wo, g[4], xp)
        (qs,) = norm_proj(xs, g[3], [dict(w=wq, scale=XA_SCALE)])
        as_ = xattn_core(qs, mem_k_rows, mem_v_rows, i, ss)
        xs = proj_norm_res(as_, wo, g[4], xs)

        wg, wu, wd = ffn_w_gate[i].astype(BF16), ffn_w_up[i].astype(BF16), ffn_w_down[i].astype(BF16)
        xp, conv_p = conv_ffn(xp, g[5], g[6], wg, wu, ffn_conv_w[i], ffn_conv_b[i], wd,
                              jnp.zeros((bp, 2, ff), F32), sp)
        xs, conv_s = conv_ffn(xs, g[5], g[6], wg, wu, ffn_conv_w[i], ffn_conv_b[i], wd, state_ffn_conv[i], ss)
        outs["p_conv"].append(conv_p); outs["s_conv"].append(conv_s)

    st = {name: jnp.stack(v) for name, v in outs.items()}
    return (xp.reshape(bp, sp, d), xs.reshape(bs, ss, d),
            st["p_sb_k"], st["p_sb_v"], st["p_ckv"], st["p_kr"], st["p_C"], st["p_n"], st["p_m"], st["p_conv"],
            st["p_mk"], st["p_mv"],
            st["s_sb_k"], st["s_sb_v"], st["s_ckv"], st["s_kr"], st["s_C"], st["s_n"], st["s_m"], st["s_conv"])
```

```python
import functools

import jax
import jax.numpy as jnp
from jax import lax
from jax.experimental import pallas as pl
from jax.experimental.pallas import tpu as pltpu

F32 = jnp.float32
BF16 = jnp.bfloat16

D_MODEL = 1024
EPS = 1e-6
N_MIXERS = 3
PAGE = 128

SB_HEADS = 16
SB_KV_HEADS = 4
SB_GROUP = SB_HEADS // SB_KV_HEADS
SB_HEAD_DIM = D_MODEL // SB_HEADS
SB_SCALE = SB_HEAD_DIM ** -0.5
SB_EXIT = 60.0

ML_HEADS = 4
ML_DV = D_MODEL // ML_HEADS
ML_DK = ML_DV // 2
ML_QK = ML_HEADS * ML_DK
ML_CHUNK = 128

MLA_HEADS = 8
MLA_NOPE = 128
MLA_ROPE = 64
MLA_VDIM = 128
MLA_KV_LORA = 512
MLA_SCALE = (MLA_NOPE + MLA_ROPE) ** -0.5
MLA_GROUP_HEADS = 8
MLA_PAGES_PER_GROUP = 8
ROPE_THETA = 10000.0

N_MEM = 256
XA_HEADS = 4
XA_HEAD_DIM = D_MODEL // XA_HEADS
XA_SCALE = XA_HEAD_DIM ** -0.5

D_FF = 2816
FFN_CHUNK = 256

NEG_BIG = -1e30
VMEM_LIMIT_BYTES = 52 * 1024 * 1024


def _cparams(*sem):
    return pltpu.CompilerParams(dimension_semantics=sem, vmem_limit_bytes=VMEM_LIMIT_BYTES)


def _rms(x, g_row):
    ms = jnp.mean(x * x, axis=-1, keepdims=True)
    return x * lax.rsqrt(ms + EPS) * g_row


def _dot(a, b):
    return jnp.dot(a, b, preferred_element_type=F32)


def _dot_nt(a, b):
    return lax.dot_general(a, b, (((1,), (1,)), ((), ())), preferred_element_type=F32)


def _dot_tn(a, b):
    return lax.dot_general(a, b, (((0,), (0,)), ((), ())), preferred_element_type=F32)


def _row_tile(m, pref):
    t = min(pref, m)
    assert m % t == 0, (m, t)
    return t


def _norm_proj_kernel(plan, x_ref, g_ref, *refs):
    n_in = sum(1 + (3 if o["rope"] else 0) + (1 if o["post"] else 0) for o in plan)
    in_refs, out_refs = refs[:n_in], refs[n_in:]
    xn = _rms(x_ref[...], g_ref[...]).astype(BF16)
    p = 0
    for o, o_ref in zip(plan, out_refs):
        y = _dot(xn, in_refs[p][...])
        p += 1
        if o["rope"]:
            y = y * in_refs[p + 1][...] + _dot(xn, in_refs[p][...]) * in_refs[p + 2][...]
            p += 3
        if o["post"]:
            y = _rms(y, in_refs[p][...])
            p += 1
        if o["scale"] != 1.0:
            y = y * o["scale"]
        o_ref[...] = y.astype(o_ref.dtype)


def norm_proj(x, g, outs, tm_pref=None):
    m, k = x.shape
    if tm_pref is None:
        tm_pref = 1024 if sum(o["w"].shape[1] for o in outs) <= 2048 else 512
    tm = _row_tile(m, tm_pref)
    plan, args, in_specs, out_shapes, out_specs = [], [x, g.reshape(1, k)], [], [], []
    in_specs.append(pl.BlockSpec((tm, k), lambda i: (i, 0)))
    in_specs.append(pl.BlockSpec((1, k), lambda i: (0, 0)))
    for o in outs:
        w = o["w"]
        n = w.shape[1]
        rope, post = o.get("rope"), o.get("post")
        plan.append(dict(rope=rope is not None, post=post is not None, scale=float(o.get("scale", 1.0))))
        args.append(w)
        in_specs.append(pl.BlockSpec((k, n), lambda i: (0, 0)))
        if rope is not None:
            w_sw, cos, sin = rope
            period = cos.shape[0] // tm
            assert cos.shape[0] % tm == 0
            args += [w_sw, cos, sin]
            in_specs.append(pl.BlockSpec((k, n), lambda i: (0, 0)))
            in_specs.append(pl.BlockSpec((tm, n), lambda i, period=period: (i % period, 0)))
            in_specs.append(pl.BlockSpec((tm, n), lambda i, period=period: (i % period, 0)))
        if post is not None:
            args.append(post.reshape(1, n))
            in_specs.append(pl.BlockSpec((1, n), lambda i: (0, 0)))
        out_shapes.append(jax.ShapeDtypeStruct((m, n), o.get("dtype", F32)))
        out_specs.append(pl.BlockSpec((tm, n), lambda i: (i, 0)))
    return pl.pallas_call(
        functools.partial(_norm_proj_kernel, plan),
        grid=(m // tm,),
        in_specs=in_specs,
        out_specs=out_specs,
        out_shape=out_shapes,
        compiler_params=_cparams("parallel"),
        name="norm_proj",
    )(*args)


def _proj_norm_res_kernel(h_ref, w_ref, g_ref, x_ref, o_ref):
    y = _dot(h_ref[...].astype(BF16), w_ref[...])
    o_ref[...] = x_ref[...] + _rms(y, g_ref[...])


def proj_norm_res(h, w, g, x, tm_pref=1024):
    m, k = h.shape
    d = w.shape[1]
    tm = _row_tile(m, tm_pref)
    return pl.pallas_call(
        _proj_norm_res_kernel,
        grid=(m // tm,),
        in_specs=[
            pl.BlockSpec((tm, k), lambda i: (i, 0)),
            pl.BlockSpec((k, d), lambda i: (0, 0)),
            pl.BlockSpec((1, d), lambda i: (0, 0)),
            pl.BlockSpec((tm, d), lambda i: (i, 0)),
        ],
        out_specs=pl.BlockSpec((tm, d), lambda i: (i, 0)),
        out_shape=jax.ShapeDtypeStruct((m, d), F32),
        compiler_params=_cparams("parallel"),
        name="proj_norm_res",
    )(h, w, g.reshape(1, d), x)


def _gelu_tanh(x):
    c = 0.7978845608028654
    return 0.5 * x * (1.0 + jnp.tanh(c * (x + 0.044715 * (x * x * x))))


def _ffn_kernel(long_seq, tiles_per_seq, nf, x_ref, g5_ref, wg_ref, wu_ref, cw_ref, cb_ref, wd_ref, g6_ref,
                *refs):
    if long_seq:
        state_ref, o_ref, tail_ref, xn_ref, acc_ref, carry_ref = refs
    else:
        p1_ref, p2_ref, o_ref, gfull_ref, xn_ref, acc_ref = refs
    i = pl.program_id(0)
    f = pl.program_id(1)

    @pl.when(f == 0)
    def _():
        xn_ref[...] = _rms(x_ref[...], g5_ref[...]).astype(BF16)
        acc_ref[...] = jnp.zeros_like(acc_ref)

    xn = xn_ref[...]
    g = _dot(xn, wg_ref[...])
    u = _dot(xn, wu_ref[...])
    tm, fc = g.shape
    rows = lax.broadcasted_iota(jnp.int32, (tm, fc), 0)
    r1 = pltpu.roll(g, 1, 0)
    r2 = pltpu.roll(g, 2, 0)
    if long_seq:
        @pl.when(i % tiles_per_seq == 0)
        def _():
            carry_ref[f, 6:8, :] = state_ref[0]

        c0 = carry_ref[f, 6:7, :]
        c1 = carry_ref[f, 7:8, :]
        prev1 = jnp.where(rows == 0, c1, r1)
        prev2 = jnp.where(rows == 0, c0, jnp.where(rows == 1, c1, r2))
        carry_ref[f] = g[tm - 8:, :]
        tail_ref[0] = g[tm - 8:, :]
    else:
        pos = rows % 8
        prev1 = jnp.where(pos == 0, p1_ref[...], r1)
        prev2 = jnp.where(pos <= 1, p2_ref[...], r2)
        gfull_ref[...] = g
    gc = cb_ref[...] + cw_ref[0:1, :] * prev2 + cw_ref[1:2, :] * prev1 + cw_ref[2:3, :] * g
    act = (_gelu_tanh(gc) * u).astype(BF16)
    acc_ref[...] += _dot(act, wd_ref[...])

    @pl.when(f == nf - 1)
    def _():
        o_ref[...] = x_ref[...] + _rms(acc_ref[...], g6_ref[...])


def conv_ffn(x, g5, g6, wg, wu, cw, cb, wd, state, seq_len, tm_pref=1024):
    m, d = x.shape
    ff = wg.shape[1]
    fc = FFN_CHUNK
    nf = ff // fc
    nseq = m // seq_len
    long_seq = seq_len % 8 == 0 and seq_len > 8
    common_specs = [
        None,
        pl.BlockSpec((1, d), lambda i, f: (0, 0)),
        pl.BlockSpec((d, fc), lambda i, f: (0, f)),
        pl.BlockSpec((d, fc), lambda i, f: (0, f)),
        pl.BlockSpec((3, fc), lambda i, f: (0, f)),
        pl.BlockSpec((1, fc), lambda i, f: (0, f)),
        pl.BlockSpec((fc, d), lambda i, f: (f, 0)),
        pl.BlockSpec((1, d), lambda i, f: (0, 0)),
    ]
    if long_seq:
        tm = _row_tile(seq_len, tm_pref)
        tps = seq_len // tm
        nm = m // tm
        common_specs[0] = pl.BlockSpec((tm, d), lambda i, f: (i, 0))
        in_specs = common_specs + [pl.BlockSpec((1, 2, fc), lambda i, f: (i // tps, 0, f))]
        out_specs = [pl.BlockSpec((tm, d), lambda i, f: (i, 0)),
                     pl.BlockSpec((1, 8, fc), lambda i, f: (i, 0, f))]
        out_shape = [jax.ShapeDtypeStruct((m, d), F32), jax.ShapeDtypeStruct((nm, 8, ff), F32)]
        scratch = [pltpu.VMEM((tm, d), BF16), pltpu.VMEM((tm, d), F32), pltpu.VMEM((nf, 8, fc), F32)]
        kern = functools.partial(_ffn_kernel, True, tps, nf)
        args = (x, g5.reshape(1, d), wg, wu, cw, cb.reshape(1, ff), wd, g6.reshape(1, d), state)
        y, tails = pl.pallas_call(
            kern, grid=(nm, nf), in_specs=in_specs, out_specs=out_specs, out_shape=out_shape,
            scratch_shapes=scratch, compiler_params=_cparams("arbitrary", "arbitrary"), name="conv_ffn_long",
        )(*args)
        new_state = tails.reshape(nseq, tps, 8, ff)[:, -1, 6:8, :]
        return y, new_state
    assert seq_len == 8
    tm = _row_tile(m, tm_pref)
    nm = m // tm
    zeros = jnp.zeros((nseq, 8, ff), F32)
    p1 = zeros.at[:, 0].set(state[:, 1]).reshape(m, ff)
    p2 = zeros.at[:, 0].set(state[:, 0]).at[:, 1].set(state[:, 1]).reshape(m, ff)
    common_specs[0] = pl.BlockSpec((tm, d), lambda i, f: (i, 0))
    in_specs = common_specs + [pl.BlockSpec((tm, fc), lambda i, f: (i, f)), pl.BlockSpec((tm, fc), lambda i, f: (i, f))]
    out_specs = [pl.BlockSpec((tm, d), lambda i, f: (i, 0)), pl.BlockSpec((tm, fc), lambda i, f: (i, f))]
    out_shape = [jax.ShapeDtypeStruct((m, d), F32), jax.ShapeDtypeStruct((m, ff), F32)]
    scratch = [pltpu.VMEM((tm, d), BF16), pltpu.VMEM((tm, d), F32)]
    kern = functools.partial(_ffn_kernel, False, 1, nf)
    y, gfull = pl.pallas_call(
        kern, grid=(nm, nf), in_specs=in_specs, out_specs=out_specs, out_shape=out_shape,
        scratch_shapes=scratch, compiler_params=_cparams("parallel", "arbitrary"), name="conv_ffn_short",
    )(x, g5.reshape(1, d), wg, wu, cw, cb.reshape(1, ff), wd, g6.reshape(1, d), p1, p2)
    new_state = gfull.reshape(nseq, 8, ff)[:, 6:8, :]
    return y, new_state


def swap_halves(w, width):
    k, n = w.shape
    half = width // 2
    w3 = w.reshape(k, n // width, width)
    return jnp.concatenate([w3[..., half:], w3[..., :half]], axis=-1).reshape(k, n)


def rope_tables(pos, n_heads):
    half = MLA_ROPE // 2
    inv = ROPE_THETA ** (-jnp.arange(half, dtype=F32) / half)
    ang = pos.astype(F32)[:, None] * inv[None, :]
    cos, sin = jnp.cos(ang), jnp.sin(ang)
    cos2 = jnp.concatenate([cos, cos], axis=-1)
    sin2 = jnp.concatenate([-sin, sin], axis=-1)
    return jnp.tile(cos2, (1, n_heads)), jnp.tile(sin2, (1, n_heads))


LANES = 128


def mem_tile_rows(c):
    nl, nb, n, nh, hd = c.shape
    tiles = hd // LANES
    c = c.reshape(nl, nb, n, nh, tiles, LANES)
    return jnp.transpose(c, (0, 1, 2, 4, 3, 5)).reshape(nl, nb, n * tiles * nh, LANES)


def _xattn_kernel(tile_rows, n_mem, seqs, q_ref, k_ref, v_ref, o_ref):
    hd = XA_HEAD_DIM
    tiles = hd // LANES
    group = tiles * XA_HEADS
    sq = q_ref.shape[0] // seqs

    def head_slab(ref, j, h):
        if not tile_rows:
            return ref[j, :, h * hd:(h + 1) * hd]
        return jnp.concatenate(
            [ref[0, j, pl.ds(t * XA_HEADS + h, n_mem, stride=group), :] for t in range(tiles)], axis=-1)

    for j in range(seqs):
        rows = slice(j * sq, (j + 1) * sq)
        for h in range(XA_HEADS):
            sl = slice(h * hd, (h + 1) * hd)
            qh = q_ref[rows, sl].astype(BF16)
            kh = head_slab(k_ref, j, h).astype(BF16)
            vh = head_slab(v_ref, j, h).astype(BF16)
            s = _dot_nt(qh, kh)
            p = jnp.exp(s - jnp.max(s, axis=-1, keepdims=True))
            l = jnp.sum(p, axis=-1, keepdims=True)
            o_ref[rows, sl] = (_dot(p.astype(BF16), vh) / l).astype(o_ref.dtype)


def xattn_core(q, k, v, layer, seq_len, tq_pref=512, seqs_per_step=4):
    m, d = q.shape
    tq = _row_tile(seq_len, tq_pref)
    tpb = seq_len // tq
    seqs = seqs_per_step if (tpb == 1 and (m // seq_len) % seqs_per_step == 0) else 1
    tile_rows = k.ndim == 4
    if tile_rows:
        rows = k.shape[2]
        n_mem = rows * LANES // d
        kv_spec = pl.BlockSpec((1, seqs, rows, LANES), lambda i: (layer, i // tpb, 0, 0))
    else:
        n_mem = k.shape[1]
        kv_spec = pl.BlockSpec((seqs, n_mem, d), lambda i: (i // tpb, 0, 0))
    tq = tq * seqs
    return pl.pallas_call(
        functools.partial(_xattn_kernel, tile_rows, n_mem, seqs),
        grid=(m // tq,),
        in_specs=[pl.BlockSpec((tq, d), lambda i: (i, 0)), kv_spec, kv_spec],
        out_specs=pl.BlockSpec((tq, d), lambda i: (i, 0)),
        out_shape=jax.ShapeDtypeStruct((m, d), q.dtype),
        compiler_params=_cparams("parallel"),
        name="xattn_core",
    )(q, k, v)


def _sb_tri(width=PAGE):
    j = lax.broadcasted_iota(jnp.int32, (2 * width, 2 * width), 0) % width
    s = lax.broadcasted_iota(jnp.int32, (2 * width, 2 * width), 1)
    return jnp.where((s >= width) | (j > s), 1.0, 0.0).astype(BF16)


def _sb_block(z, valid, r, tri):
    width = z.shape[1]
    lsz = jnp.minimum(z, 0.0) - jnp.log(1.0 + jnp.exp(-jnp.abs(z)))
    ls = lsz - z
    if valid is not None:
        ls = jnp.where(valid, ls, 0.0)
    hi = ls.astype(BF16)
    lo = (ls - hi.astype(F32)).astype(BF16)
    cum = _dot(jnp.concatenate([hi, lo], axis=-1), tri)
    a = jnp.exp(lsz + cum[:, :width] + _lane_tile(r, width))
    if valid is not None:
        a = jnp.where(valid, a, 0.0)
    return a.astype(BF16), r + cum[:, width:width + PAGE]


def _sb_prompt_kernel(q_ref, k_ref, v_ref, o_ref, kh_ref, vh_ref, qs_ref, r_ref, acc_ref):
    qi = pl.program_id(1)
    hd, grp, nkv = SB_HEAD_DIM, SB_GROUP, SB_KV_HEADS
    tq = q_ref.shape[0]

    @pl.when(qi == 0)
    def _():
        for h in range(nkv):
            kh_ref[h] = k_ref[:, h * hd:(h + 1) * hd].astype(BF16)
            vh_ref[h] = v_ref[:, h * hd:(h + 1) * hd].astype(BF16)

    for h in range(nkv):
        for g in range(grp):
            c = (h * grp + g) * hd
            qs_ref[h, g * tq:(g + 1) * tq, :] = q_ref[:, c:c + hd].astype(BF16)
    r_ref[...] = jnp.zeros_like(r_ref)
    acc_ref[...] = jnp.zeros_like(acc_ref)
    tri = _sb_tri()
    rows_t = lax.broadcasted_iota(jnp.int32, (grp * tq, PAGE), 0) % tq
    cols = lax.broadcasted_iota(jnp.int32, (grp * tq, PAGE), 1)

    def sweep(kb, valid):
        start = pl.multiple_of(kb * PAGE, PAGE)
        rmax = jnp.float32(-jnp.inf)
        for h in range(nkv):
            a, rn = _sb_block(_dot_nt(qs_ref[h], kh_ref[h, pl.ds(start, PAGE), :]), valid, r_ref[h], tri)
            acc_ref[h] += _dot(a, vh_ref[h, pl.ds(start, PAGE), :])
            r_ref[h] = rn
            rmax = jnp.maximum(rmax, jnp.max(rn))
        return rmax

    def cond(c):
        j, rmax = c
        return jnp.logical_and(j <= qi, rmax > -SB_EXIT)

    def body(c):
        j, _ = c
        return j + 1, sweep(qi - j, None)

    lax.while_loop(cond, body, (jnp.int32(1), sweep(qi, cols < rows_t)))
    o_ref[...] = jnp.concatenate(
        [acc_ref[h, g * tq:(g + 1) * tq, :] for h in range(nkv) for g in range(grp)], axis=-1).astype(o_ref.dtype)


def sb_attention_prompt(q, k, v, seq_len):
    m = q.shape[0]
    tq = PAGE
    nq = seq_len // tq
    nb = m // seq_len
    nkv, hd, grp = SB_KV_HEADS, SB_HEAD_DIM, SB_GROUP
    return pl.pallas_call(
        _sb_prompt_kernel,
        grid=(nb, nq),
        in_specs=[
            pl.BlockSpec((tq, SB_HEADS * hd), lambda b, i: (b * nq + i, 0)),
            pl.BlockSpec((seq_len, nkv * hd), lambda b, i: (b, 0)),
            pl.BlockSpec((seq_len, nkv * hd), lambda b, i: (b, 0)),
        ],
        out_specs=pl.BlockSpec((tq, SB_HEADS * hd), lambda b, i: (b * nq + i, 0)),
        out_shape=jax.ShapeDtypeStruct((m, SB_HEADS * hd), BF16),
        scratch_shapes=[
            pltpu.VMEM((nkv, seq_len, hd), BF16),
            pltpu.VMEM((nkv, seq_len, hd), BF16),
            pltpu.VMEM((nkv, grp * tq, hd), BF16),
            pltpu.VMEM((nkv, grp * tq, PAGE), F32),
            pltpu.VMEM((nkv, grp * tq, hd), F32),
        ],
        compiler_params=_cparams("parallel", "arbitrary"),
        name="sb_attention_prompt",
    )(q, k, v)


def _sb_sample_kernel(layer, n_pages, pt_ref, q_ref, kn_ref, vn_ref, kpool, vpool, o_ref,
                      kbuf, vbuf, sem, kfb, vfb, semfb, qs_ref, r_ref, acc_ref, knp_ref, vnp_ref):
    b = pl.program_id(0)
    nb = pl.num_programs(0)
    hd, grp, nkv = SB_HEAD_DIM, SB_GROUP, SB_KV_HEADS
    sq = q_ref.shape[0]
    last = n_pages - 1

    def newest_page(bb, slot):
        page = pt_ref[bb, last]
        return (pltpu.make_async_copy(kpool.at[layer, page], kbuf.at[slot], sem.at[slot, 0]),
                pltpu.make_async_copy(vpool.at[layer, page], vbuf.at[slot], sem.at[slot, 1]))

    @pl.when(b == 0)
    def _():
        for c in newest_page(0, 0):
            c.start()

    @pl.when(b + 1 < nb)
    def _():
        for c in newest_page(b + 1, (b + 1) % 2):
            c.start()

    for h in range(nkv):
        for g in range(grp):
            c = (h * grp + g) * hd
            qs_ref[h, g * sq:(g + 1) * sq, :] = q_ref[:, c:c + hd]
    r_ref[...] = jnp.zeros_like(r_ref)
    acc_ref[...] = jnp.zeros_like(acc_ref)
    knp_ref[...] = jnp.zeros_like(knp_ref)
    vnp_ref[...] = jnp.zeros_like(vnp_ref)
    knp_ref[0:sq, :] = kn_ref[...]
    vnp_ref[0:sq, :] = vn_ref[...]
    rows_t = lax.broadcasted_iota(jnp.int32, (grp * sq, PAGE), 0) % sq
    cols = lax.broadcasted_iota(jnp.int32, (grp * sq, PAGE), 1)

    def sweep(logits, weighted_values, valid, tri):
        rmax = jnp.float32(-jnp.inf)
        for h in range(nkv):
            a, rn = _sb_block(logits(h, qs_ref[h].astype(BF16)), valid, r_ref[h], tri)
            acc_ref[h] += weighted_values(h, a)
            r_ref[h] = rn
            rmax = jnp.maximum(rmax, jnp.max(rn))
        return rmax

    def sweep_page(k_ref, v_ref):
        return sweep(lambda h, qs: _dot(qs, k_ref[h].astype(BF16)),
                     lambda h, a: _dot_nt(a, v_ref[h].astype(BF16)), None, _sb_tri())

    slot = b % 2
    for c in newest_page(b, slot):
        c.wait()

    valid = jnp.concatenate([cols >= 0, cols < rows_t], axis=-1)

    def first_logits(h, qs):
        return jnp.concatenate([_dot(qs, kbuf[slot, h].astype(BF16)),
                                _dot_nt(qs, knp_ref[:, h * hd:(h + 1) * hd].astype(BF16))], axis=-1)

    def first_values(h, a):
        return (_dot_nt(a[:, :PAGE], vbuf[slot, h].astype(BF16))
                + _dot(a[:, PAGE:], vnp_ref[:, h * hd:(h + 1) * hd].astype(BF16)))

    rmax = sweep(first_logits, first_values, valid, _sb_tri(2 * PAGE))

    def cond(c):
        p, rm = c
        return jnp.logical_and(p >= 0, rm > -SB_EXIT)

    def body(c):
        p, _ = c
        page = pt_ref[b, p]
        ck = pltpu.make_async_copy(kpool.at[layer, page], kfb, semfb.at[0])
        cv = pltpu.make_async_copy(vpool.at[layer, page], vfb, semfb.at[1])
        ck.start()
        cv.start()
        ck.wait()
        cv.wait()
        return p - 1, sweep_page(kfb, vfb)

    lax.while_loop(cond, body, (jnp.int32(last - 1), rmax))
    o_ref[...] = jnp.concatenate(
        [acc_ref[h, g * sq:(g + 1) * sq, :] for h in range(nkv) for g in range(grp)], axis=-1).astype(o_ref.dtype)


def sb_attention_sample(q, k_new, v_new, k_pool, v_pool, layer, page_table, seq_len):
    m = q.shape[0]
    nb, n_pages = page_table.shape
    assert m == nb * seq_len and seq_len == 8
    nkv, hd, grp = SB_KV_HEADS, SB_HEAD_DIM, SB_GROUP
    grid_spec = pltpu.PrefetchScalarGridSpec(
        num_scalar_prefetch=1,
        grid=(nb,),
        in_specs=[
            pl.BlockSpec((seq_len, SB_HEADS * hd), lambda b, pt: (b, 0)),
            pl.BlockSpec((seq_len, nkv * hd), lambda b, pt: (b, 0)),
            pl.BlockSpec((seq_len, nkv * hd), lambda b, pt: (b, 0)),
            pl.BlockSpec(memory_space=pl.ANY),
            pl.BlockSpec(memory_space=pl.ANY),
        ],
        out_specs=pl.BlockSpec((seq_len, SB_HEADS * hd), lambda b, pt: (b, 0)),
        scratch_shapes=[
            pltpu.VMEM((2, nkv, hd, PAGE), F32),
            pltpu.VMEM((2, nkv, hd, PAGE), F32),
            pltpu.SemaphoreType.DMA((2, 2)),
            pltpu.VMEM((nkv, hd, PAGE), F32),
            pltpu.VMEM((nkv, hd, PAGE), F32),
            pltpu.SemaphoreType.DMA((2,)),
            pltpu.VMEM((nkv, grp * seq_len, hd), F32),
            pltpu.VMEM((nkv, grp * seq_len, PAGE), F32),
            pltpu.VMEM((nkv, grp * seq_len, hd), F32),
            pltpu.VMEM((PAGE, nkv * hd), F32),
            pltpu.VMEM((PAGE, nkv * hd), F32),
        ],
    )
    return pl.pallas_call(
        functools.partial(_sb_sample_kernel, layer, n_pages),
        grid_spec=grid_spec,
        out_shape=jax.ShapeDtypeStruct((m, SB_HEADS * hd), F32),
        compiler_params=_cparams("arbitrary"),
        name="sb_attention_sample",
    )(page_table, q, k_new, v_new, k_pool, v_pool)


def _mla_load_queries(qn_ref, qr_ref, wuk_ref, ql_ref, qr_s_ref):
    tq = qn_ref.shape[0]
    for h in range(MLA_HEADS):
        qn = qn_ref[:, h * MLA_NOPE:(h + 1) * MLA_NOPE].astype(BF16)
        ql_ref[h * tq:(h + 1) * tq, :] = (_dot_nt(qn, wuk_ref[h]) * MLA_SCALE).astype(ql_ref.dtype)
        qr_s_ref[h * tq:(h + 1) * tq, :] = (qr_ref[:, h * MLA_ROPE:(h + 1) * MLA_ROPE] * MLA_SCALE).astype(qr_s_ref.dtype)


def _lane_tile(x, width):
    return jnp.concatenate([x] * (width // LANES), axis=-1)


def _mla_step(s, ck, valid, m_ref, l_ref, acc_ref, rows):
    if valid is not None:
        s = jnp.where(valid, s, NEG_BIG)
    m_prev = m_ref[rows, :]
    m_new = jnp.maximum(m_prev, jnp.max(s, axis=-1, keepdims=True))
    alpha = jnp.exp(m_prev - m_new)
    p = jnp.exp(s - _lane_tile(m_new, s.shape[1]))
    l_ref[rows, :] = alpha * l_ref[rows, :] + jnp.sum(p, axis=-1, keepdims=True)
    acc_ref[rows, :] = _lane_tile(alpha, acc_ref.shape[1]) * acc_ref[rows, :] + _dot(p.astype(BF16), ck)
    m_ref[rows, :] = m_new


def _mla_store_output(wuv_ref, l_ref, acc_ref, o_ref):
    tq = o_ref.shape[0]
    for h in range(MLA_HEADS):
        rows = slice(h * tq, (h + 1) * tq)
        ol = (acc_ref[rows, :] / _lane_tile(l_ref[rows, :], acc_ref.shape[1])).astype(BF16)
        o_ref[:, h * MLA_VDIM:(h + 1) * MLA_VDIM] = _dot(ol, wuv_ref[h]).astype(o_ref.dtype)


def _mla_prompt_kernel(tk, qn_ref, qr_ref, ckv_ref, kr_ref, wuk_ref, wuv_ref, o_ref,
                       ck_s, kr_s, ql_s, qr_s, m_s, l_s, acc_s):
    qi = pl.program_id(1)
    tq = qn_ref.shape[0]
    rows_n = MLA_HEADS * tq

    @pl.when(qi == 0)
    def _():
        ck_s[...] = ckv_ref[...].astype(BF16)
        kr_s[...] = kr_ref[...].astype(BF16)

    _mla_load_queries(qn_ref, qr_ref, wuk_ref, ql_s, qr_s)
    m_s[...] = jnp.full_like(m_s, NEG_BIG)
    l_s[...] = jnp.zeros_like(l_s)
    acc_s[...] = jnp.zeros_like(acc_s)
    grp_rows = MLA_GROUP_HEADS * tq
    q_pos = qi * tq + lax.broadcasted_iota(jnp.int32, (grp_rows, tk), 0) % tq
    cols = lax.broadcasted_iota(jnp.int32, (grp_rows, tk), 1)

    def block(kb, masked):
        start = pl.multiple_of(kb * tk, tk)
        ck = ck_s[pl.ds(start, tk), :]
        kr = kr_s[pl.ds(start, tk), :]
        valid = (cols + kb * tk <= q_pos) if masked else None
        for r0 in range(0, rows_n, grp_rows):
            rows = slice(r0, r0 + grp_rows)
            s = _dot_nt(ql_s[rows, :], ck) + _dot_nt(qr_s[rows, :], kr)
            _mla_step(s, ck, valid, m_s, l_s, acc_s, rows)

    def body(kb, carry):
        block(kb, False)
        return carry

    n_blocks = ((qi + 1) * tq + tk - 1) // tk
    lax.fori_loop(0, n_blocks - 1, body, 0)
    block(n_blocks - 1, True)
    _mla_store_output(wuv_ref, l_s, acc_s, o_ref)


def mla_attention_prompt(qn, qr, ckv, kr, wuk, wuv, seq_len):
    m = qn.shape[0]
    tq = PAGE
    tk = next(t * PAGE for t in (4, 2, 1) if seq_len % (t * PAGE) == 0)
    nq = seq_len // tq
    nb = m // seq_len
    rows_n = MLA_HEADS * tq
    lat = MLA_KV_LORA
    return pl.pallas_call(
        functools.partial(_mla_prompt_kernel, tk),
        grid=(nb, nq),
        in_specs=[
            pl.BlockSpec((tq, MLA_HEADS * MLA_NOPE), lambda b, i: (b * nq + i, 0)),
            pl.BlockSpec((tq, MLA_HEADS * MLA_ROPE), lambda b, i: (b * nq + i, 0)),
            pl.BlockSpec((seq_len, lat), lambda b, i: (b, 0)),
            pl.BlockSpec((seq_len, MLA_ROPE), lambda b, i: (b, 0)),
            pl.BlockSpec((MLA_HEADS, lat, MLA_NOPE), lambda b, i: (0, 0, 0)),
            pl.BlockSpec((MLA_HEADS, lat, MLA_VDIM), lambda b, i: (0, 0, 0)),
        ],
        out_specs=pl.BlockSpec((tq, MLA_HEADS * MLA_VDIM), lambda b, i: (b * nq + i, 0)),
        out_shape=jax.ShapeDtypeStruct((m, MLA_HEADS * MLA_VDIM), BF16),
        scratch_shapes=[
            pltpu.VMEM((seq_len, lat), BF16),
            pltpu.VMEM((seq_len, MLA_ROPE), BF16),
            pltpu.VMEM((rows_n, lat), BF16),
            pltpu.VMEM((rows_n, MLA_ROPE), BF16),
            pltpu.VMEM((rows_n, LANES), F32),
            pltpu.VMEM((rows_n, LANES), F32),
            pltpu.VMEM((rows_n, lat), F32),
        ],
        compiler_params=_cparams("parallel", "arbitrary"),
        name="mla_attention_prompt",
    )(qn, qr, ckv, kr, wuk, wuv)


def _mla_sample_kernel(layer, n_pages, ppi, pt_ref, qn_ref, qr_ref, cn_ref, rn_ref, cpool, rpool, wuk_ref, wuv_ref, o_ref,
                       cbuf, rbuf, sem, cnp_s, rnp_s, ql_s, qr_s, ck_all, s_all):
    b = pl.program_id(0)
    nb = pl.num_programs(0)
    sq = qn_ref.shape[0]
    rows_n = MLA_HEADS * sq
    n_it = n_pages // ppi

    def group_copies(bb, it, slot):
        out = []
        for j in range(ppi):
            page = pt_ref[bb, it * ppi + j]
            out.append(pltpu.make_async_copy(cpool.at[layer, page], cbuf.at[slot, pl.ds(j * PAGE, PAGE), :],
                                             sem.at[slot, 0, j]))
            out.append(pltpu.make_async_copy(rpool.at[layer, page], rbuf.at[slot, j], sem.at[slot, 1, j]))
        return out

    n_slots = cbuf.shape[0]
    total = nb * n_it

    def start_group(g):
        for c in group_copies(g // n_it, g % n_it, g % n_slots):
            c.start()

    @pl.when(b == 0)
    def _():
        for g in range(n_slots - 1):
            if g < total:
                start_group(g)

    _mla_load_queries(qn_ref, qr_ref, wuk_ref, ql_s, qr_s)
    cnp_s[...] = jnp.zeros_like(cnp_s)
    rnp_s[...] = jnp.zeros_like(rnp_s)
    cnp_s[0:sq, :] = cn_ref[...]
    rnp_s[0:sq, :] = rn_ref[...]
    rows_t = lax.broadcasted_iota(jnp.int32, (rows_n, PAGE), 0) % sq
    cols = lax.broadcasted_iota(jnp.int32, (rows_n, PAGE), 1)
    ql = ql_s[...].astype(BF16)
    qr = qr_s[...].astype(BF16)
    cn = cnp_s[...].astype(BF16)
    s_new = jnp.where(cols <= rows_t, _dot_nt(ql, cn) + _dot_nt(qr, rnp_s[...].astype(BF16)), NEG_BIG)

    def body(it, carry):
        g = b * n_it + it
        slot = g % n_slots

        @pl.when(g + n_slots - 1 < total)
        def _():
            start_group(g + n_slots - 1)

        for c in group_copies(b, it, slot):
            c.wait()
        ck = cbuf[slot].astype(BF16)
        ck_all[it] = ck
        s_rope = jnp.concatenate([_dot(qr, rbuf[slot, j].astype(BF16)) for j in range(ppi)], axis=-1)
        s_all[it] = _dot_nt(ql, ck) + s_rope
        return carry

    lax.fori_loop(0, n_it, body, 0)

    m_run = jnp.max(s_new, axis=-1, keepdims=True)
    for it in range(n_it):
        m_run = jnp.maximum(m_run, jnp.max(s_all[it], axis=-1, keepdims=True))
    p_new = jnp.exp(s_new - m_run)
    l_run = jnp.sum(p_new, axis=-1, keepdims=True)
    acc = _dot(p_new.astype(BF16), cn)
    for it in range(n_it):
        p = jnp.exp(s_all[it] - m_run)
        l_run = l_run + jnp.sum(p, axis=-1, keepdims=True)
        acc = acc + _dot(p.astype(BF16), ck_all[it])
    acc = acc / l_run
    for h in range(MLA_HEADS):
        ol = acc[h * sq:(h + 1) * sq, :].astype(BF16)
        o_ref[:, h * MLA_VDIM:(h + 1) * MLA_VDIM] = _dot(ol, wuv_ref[h]).astype(o_ref.dtype)


def mla_attention_sample(qn, qr, c_new, r_new, c_pool, r_pool, layer, page_table, wuk, wuv, seq_len):
    m = qn.shape[0]
    nb, n_pages = page_table.shape
    assert m == nb * seq_len and seq_len == 8
    ppi = MLA_PAGES_PER_GROUP if n_pages % MLA_PAGES_PER_GROUP == 0 else 1
    n_it = n_pages // ppi
    n_slots = 4
    rows_n = MLA_HEADS * seq_len
    lat = MLA_KV_LORA
    grid_spec = pltpu.PrefetchScalarGridSpec(
        num_scalar_prefetch=1,
        grid=(nb,),
        in_specs=[
            pl.BlockSpec((seq_len, MLA_HEADS * MLA_NOPE), lambda b, pt: (b, 0)),
            pl.BlockSpec((seq_len, MLA_HEADS * MLA_ROPE), lambda b, pt: (b, 0)),
            pl.BlockSpec((seq_len, lat), lambda b, pt: (b, 0)),
            pl.BlockSpec((seq_len, MLA_ROPE), lambda b, pt: (b, 0)),
            pl.BlockSpec(memory_space=pl.ANY),
            pl.BlockSpec(memory_space=pl.ANY),
            pl.BlockSpec((MLA_HEADS, lat, MLA_NOPE), lambda b, pt: (0, 0, 0)),
            pl.BlockSpec((MLA_HEADS, lat, MLA_VDIM), lambda b, pt: (0, 0, 0)),
        ],
        out_specs=pl.BlockSpec((seq_len, MLA_HEADS * MLA_VDIM), lambda b, pt: (b, 0)),
        scratch_shapes=[
            pltpu.VMEM((n_slots, ppi * PAGE, lat), F32),
            pltpu.VMEM((n_slots, ppi, MLA_ROPE, PAGE), F32),
            pltpu.SemaphoreType.DMA((n_slots, 2, ppi)),
            pltpu.VMEM((PAGE, lat), F32),
            pltpu.VMEM((PAGE, MLA_ROPE), F32),
            pltpu.VMEM((rows_n, lat), F32),
            pltpu.VMEM((rows_n, MLA_ROPE), F32),
            pltpu.VMEM((n_it, ppi * PAGE, lat), BF16),
            pltpu.VMEM((n_it, rows_n, ppi * PAGE), F32),
        ],
    )
    return pl.pallas_call(
        functools.partial(_mla_sample_kernel, layer, n_pages, ppi),
        grid_spec=grid_spec,
        out_shape=jax.ShapeDtypeStruct((m, MLA_HEADS * MLA_VDIM), F32),
        compiler_params=_cparams("arbitrary"),
        name="mla_attention_sample",
    )(page_table, qn, qr, c_new, r_new, c_pool, r_pool, wuk, wuv)


def _split3(x):
    a = x.astype(BF16)
    r = x - a.astype(F32)
    b = r.astype(BF16)
    c = (r - b.astype(F32)).astype(BF16)
    return a, b, c


def _mlstm_kernel(lr, nc, nbs, q_ref, k_ref, v_ref, og_ref, gt_ref, bias_ref, hg_ref, c0_ref, n0_ref, m0_ref,
                  out_ref, c_out, n_out, m_out, c_s, n_s, m_s, *pad):
    c_idx = pl.program_id(1)
    L = ML_CHUNK

    @pl.when(c_idx == 0)
    def _():
        c_s[...] = c0_ref[...]
        n_s[...] = n0_ref[...]
        m_s[...] = m0_ref[...]

    row = lax.broadcasted_iota(jnp.int32, (L, 128), 0)
    li = lax.broadcasted_iota(jnp.int32, (L, L), 0)
    si = lax.broadcasted_iota(jnp.int32, (L, L), 1)
    causal = si <= li
    tril = jnp.where(causal, 1.0, 0.0).astype(BF16)
    for jb in range(nbs):
        _mlstm_chunk(lr, jb, row, causal, tril, q_ref, k_ref, v_ref, og_ref, gt_ref, bias_ref, hg_ref,
                     out_ref, c_s, n_s, m_s, pad)

    @pl.when(c_idx == nc - 1)
    def _():
        c_out[...] = c_s[...]
        n_out[...] = n_s[...]
        m_out[...] = m_s[...]


def _mlstm_chunk(lr, jb, row, causal, tril, q_ref, k_ref, v_ref, og_ref, gt_ref, bias_ref, hg_ref,
                 out_ref, c_s, n_s, m_s, pad):
    L = ML_CHUNK
    dk, dv, nh = ML_DK, ML_DV, ML_HEADS
    if lr == L:
        q_all, k_all, v_all, gates = q_ref[jb], k_ref[jb], v_ref[jb], gt_ref[jb]
    else:
        qp, kp, vp, gp = pad
        for dst, src in ((qp, q_ref), (kp, k_ref), (vp, v_ref), (gp, gt_ref)):
            dst[jb] = jnp.zeros(dst.shape[1:], F32)
            dst[jb, 0:lr, :] = src[jb]
        q_all, k_all, v_all, gates = qp[jb], kp[jb], vp[jb], gp[jb]

    gb = gates + bias_ref[...]
    lf = jnp.minimum(gb, 0.0) - jnp.log(1.0 + jnp.exp(-jnp.abs(gb)))
    if lr != L:
        lf = jnp.where(row < lr, lf, 0.0)
        gb = jnp.where(row < lr, gb, NEG_BIG)
    t0, t1, t2 = _split3(lf)
    bcum = _dot(tril, t0) + _dot(tril, t1) + _dot(tril, t2)
    bcum_t = bcum.T
    gb_t = gb.T

    for h in range(nh):
        qh = q_all[:, h * dk:(h + 1) * dk]
        kh = k_all[:, h * dk:(h + 1) * dk] * (dk ** -0.5)
        vh = v_all[:, h * dv:(h + 1) * dv]
        qb, kb, vb = qh.astype(BF16), kh.astype(BF16), vh.astype(BF16)
        b_col = bcum[:, nh + h:nh + h + 1]
        i_col = gb[:, h:h + 1]
        b_row = bcum_t[nh + h:nh + h + 1, :]
        i_row = gb_t[h:h + 1, :]
        m_old = m_s[jb, h:h + 1, 0:1]
        n_old = n_s[jb, h:h + 1, :]
        c_old = c_s[jb, h]

        d = jnp.where(causal, b_col - b_row + i_row, NEG_BIG)
        inter = b_col + m_old
        m_q = jnp.maximum(inter, jnp.max(d, axis=-1, keepdims=True))
        sc = _dot_nt(qb, kb) * jnp.exp(d - m_q)
        w_st = jnp.exp(inter - m_q)
        num = _dot(sc.astype(BF16), vb) + w_st * _dot_nt(qb, c_old.astype(BF16))
        den = jnp.sum(sc, axis=-1, keepdims=True) + w_st * jnp.sum(qh * n_old, axis=-1, keepdims=True)
        hc = num / jnp.maximum(jnp.abs(den), jnp.exp(-m_q))

        b_end = bcum[L - 1:L, nh + h:nh + h + 1]
        g_col = b_end - b_col + i_col
        g_row = b_end - b_row + i_row
        m_new = jnp.maximum(b_end + m_old, jnp.max(g_row, axis=-1, keepdims=True))
        w_end = jnp.exp(g_col - m_new)
        decay = jnp.exp(b_end + m_old - m_new)
        c_s[jb, h] = decay * c_old + _dot_tn((vh * w_end).astype(BF16), kb)
        n_s[jb, h:h + 1, :] = decay * n_old + jnp.sum(w_end * kh, axis=0, keepdims=True)
        m_s[jb, h:h + 1, :] = jnp.broadcast_to(m_new, (1, 128))

        hn = _rms(hc, hg_ref[:, h * dv:(h + 1) * dv])
        og = og_ref[jb, :, h * dv:(h + 1) * dv]
        gate = 1.0 / (1.0 + jnp.exp(-og))
        out_ref[jb, :, h * dv:(h + 1) * dv] = (gate * hn[0:lr, :]).astype(out_ref.dtype)


def mlstm_core(q, k, v, og, gates, bias, head_g, c0, n0, m0, seq_len, out_dtype, seqs_per_step):
    m_rows = q.shape[0]
    nb = m_rows // seq_len
    lr = min(ML_CHUNK, seq_len)
    nc = seq_len // lr
    nh, dk, dv = ML_HEADS, ML_DK, ML_DV
    n0p = jnp.zeros((nb, 8, dk), F32).at[:, :nh].set(n0)
    m0p = jnp.zeros((nb, 8, 128), F32).at[:, :nh].set(jnp.broadcast_to(m0[:, :, None], (nb, nh, 128)))
    nbs = seqs_per_step if nb % seqs_per_step == 0 else 1
    row_spec = lambda w: pl.BlockSpec((nbs, lr, w), lambda b, c: (b, c, 0))
    seq_rows = lambda a: a.reshape(nb, seq_len, a.shape[1])
    scratch = [pltpu.VMEM((nbs, nh, dv, dk), F32), pltpu.VMEM((nbs, 8, dk), F32), pltpu.VMEM((nbs, 8, 128), F32)]
    if lr != ML_CHUNK:
        scratch += [pltpu.VMEM((nbs, ML_CHUNK, nh * dk), F32), pltpu.VMEM((nbs, ML_CHUNK, nh * dk), F32),
                    pltpu.VMEM((nbs, ML_CHUNK, nh * dv), F32), pltpu.VMEM((nbs, ML_CHUNK, 128), F32)]
    out, c_f, n_f, m_f = pl.pallas_call(
        functools.partial(_mlstm_kernel, lr, nc, nbs),
        grid=(nb // nbs, nc),
        in_specs=[
            row_spec(nh * dk), row_spec(nh * dk), row_spec(nh * dv), row_spec(nh * dv), row_spec(128),
            pl.BlockSpec((1, 128), lambda b, c: (0, 0)),
            pl.BlockSpec((1, nh * dv), lambda b, c: (0, 0)),
            pl.BlockSpec((nbs, nh, dv, dk), lambda b, c: (b, 0, 0, 0)),
            pl.BlockSpec((nbs, 8, dk), lambda b, c: (b, 0, 0)),
            pl.BlockSpec((nbs, 8, 128), lambda b, c: (b, 0, 0)),
        ],
        out_specs=[
            row_spec(nh * dv),
            pl.BlockSpec((nbs, nh, dv, dk), lambda b, c: (b, 0, 0, 0)),
            pl.BlockSpec((nbs, 8, dk), lambda b, c: (b, 0, 0)),
            pl.BlockSpec((nbs, 8, 128), lambda b, c: (b, 0, 0)),
        ],
        out_shape=[
            jax.ShapeDtypeStruct((nb, seq_len, nh * dv), out_dtype),
            jax.ShapeDtypeStruct((nb, nh, dv, dk), F32),
            jax.ShapeDtypeStruct((nb, 8, dk), F32),
            jax.ShapeDtypeStruct((nb, 8, 128), F32),
        ],
        scratch_shapes=scratch,
        compiler_params=_cparams("parallel", "arbitrary"),
        name="mlstm_core",
    )(seq_rows(q), seq_rows(k), seq_rows(v), seq_rows(og), seq_rows(gates), bias.reshape(1, 128),
      head_g.reshape(1, nh * dv), c0, n0p, m0p)
    return out.reshape(m_rows, nh * dv), c_f, n_f[:, :nh], m_f[:, :nh, 0]


def prep_sb(w_qkv, w_o):
    nq, nkv = SB_HEADS * SB_HEAD_DIM, SB_KV_HEADS * SB_HEAD_DIM
    return dict(wq=w_qkv[:, :nq].astype(BF16), wk=w_qkv[:, nq:nq + nkv].astype(BF16),
                wv=w_qkv[:, nq + nkv:].astype(BF16), wo=w_o.astype(BF16))


def sb_mixer(x, g, w, seq_len, past=None):
    q, k, v = norm_proj(x, g, [dict(w=w["wq"], scale=SB_SCALE), dict(w=w["wk"]), dict(w=w["wv"])])
    if past is None:
        o = sb_attention_prompt(q, k, v, seq_len)
    else:
        k_pool, v_pool, layer, page_table = past
        o = sb_attention_sample(q, k, v, k_pool, v_pool, layer, page_table, seq_len)
    return o, k, v


def prep_ml(w_in, b_gates, head_g, w_out):
    o0 = 2 * ML_QK + ML_HEADS * ML_DV
    gates_w = jnp.zeros((D_MODEL, 128), F32).at[:, :2 * ML_HEADS].set(w_in[:, o0 + D_MODEL:])
    bias = jnp.zeros((128,), F32).at[:2 * ML_HEADS].set(b_gates)
    return dict(wq=w_in[:, :ML_QK].astype(BF16), wk=w_in[:, ML_QK:2 * ML_QK].astype(BF16),
                wv=w_in[:, 2 * ML_QK:o0].astype(BF16), wog=w_in[:, o0:o0 + D_MODEL].astype(BF16),
                wgt=gates_w.astype(BF16), bias=bias, head_g=head_g.reshape(-1), wo=w_out.astype(BF16))


def ml_mixer(x, g, w, seq_len, state, out_dtype):
    q, k, v, og, gates = norm_proj(
        x, g, [dict(w=w["wq"]), dict(w=w["wk"]), dict(w=w["wv"]), dict(w=w["wog"]), dict(w=w["wgt"])])
    c0, n0, m0 = state
    seqs_per_step = 1 if seq_len >= ML_CHUNK else 4
    return mlstm_core(q, k, v, og, gates, w["bias"], w["head_g"], c0, n0, m0, seq_len, out_dtype, seqs_per_step)


def prep_mla(w_dq, g_q, w_uq, w_dkv, g_kv, w_uk, w_uv, w_o):
    per = MLA_NOPE + MLA_ROPE
    w_uq3 = w_uq.reshape(w_uq.shape[0], MLA_HEADS, per)
    w_nope = w_uq3[:, :, :MLA_NOPE].reshape(-1, MLA_HEADS * MLA_NOPE)
    w_rope = w_uq3[:, :, MLA_NOPE:].reshape(-1, MLA_HEADS * MLA_ROPE)
    w_c, w_r = w_dkv[:, :MLA_KV_LORA], w_dkv[:, MLA_KV_LORA:]
    return dict(wdq=w_dq.astype(BF16), g_q=g_q, w_nope=w_nope.astype(BF16), w_rope=w_rope.astype(BF16),
                w_rope_sw=swap_halves(w_rope, MLA_ROPE).astype(BF16), w_c=w_c.astype(BF16), g_kv=g_kv,
                w_r=w_r.astype(BF16), w_r_sw=swap_halves(w_r, MLA_ROPE).astype(BF16),
                wuk=jnp.transpose(w_uk, (1, 0, 2)).astype(BF16), wuv=jnp.transpose(w_uv, (1, 0, 2)).astype(BF16),
                wo=w_o.astype(BF16))


def mla_mixer(x, g, w, seq_len, tables, past=None):
    cos1, sin1, cos8, sin8 = tables
    cq, ckv, kr = norm_proj(x, g, [
        dict(w=w["wdq"]),
        dict(w=w["w_c"], post=w["g_kv"]),
        dict(w=w["w_r"], rope=(w["w_r_sw"], cos1, sin1)),
    ], tm_pref=cos1.shape[0] if cos1.shape[0] < 512 else 512)
    qn, qr = norm_proj(cq, w["g_q"], [
        dict(w=w["w_nope"]),
        dict(w=w["w_rope"], rope=(w["w_rope_sw"], cos8, sin8)),
    ], tm_pref=cos1.shape[0] if cos1.shape[0] < 512 else 512)
    if past is None:
        o = mla_attention_prompt(qn, qr, ckv, kr, w["wuk"], w["wuv"], seq_len)
    else:
        c_pool, r_pool, layer, page_table = past
        o = mla_attention_sample(qn, qr, ckv, kr, c_pool, r_pool, layer, page_table, w["wuk"], w["wuv"], seq_len)
    return o, ckv, kr


def kernel(x_prompt, x_sample, cache_sb_k, cache_sb_v, cache_mla_ckv, cache_mla_kr, state_ml_C, state_ml_n, state_ml_m, state_ffn_conv, cache_mem_k, cache_mem_v, page_table, mem_prompt, norm_g, sb_w_qkv, sb_w_o, ml_w_in, ml_b_gates, ml_head_g, ml_w_out, mla_w_dq, mla_g_q, mla_w_uq, mla_w_dkv, mla_g_kv, mla_w_uk, mla_w_uv, mla_w_o, xa_w_q, xa_w_kv, xa_w_o, ffn_w_gate, ffn_w_up, ffn_conv_w, ffn_conv_b, ffn_w_down):
    bp, sp, d = x_prompt.shape
    bs, ss, _ = x_sample.shape
    depth = norm_g.shape[0]
    n_pages = page_table.shape[1]
    past_len = n_pages * cache_sb_k.shape[2]
    n_mem = mem_prompt.shape[1]
    ff = ffn_w_gate.shape[2]
    sb_k_pages = jnp.transpose(cache_sb_k, (0, 1, 3, 4, 2))
    sb_v_pages = jnp.transpose(cache_sb_v, (0, 1, 3, 4, 2))
    mla_kr_pages = jnp.transpose(cache_mla_kr, (0, 1, 3, 2))
    mem_k_rows = mem_tile_rows(cache_mem_k)
    mem_v_rows = mem_tile_rows(cache_mem_v)

    xp = x_prompt.reshape(bp * sp, d)
    xs = x_sample.reshape(bs * ss, d)
    mem = mem_prompt.reshape(bp * n_mem, d)

    pos_p = jnp.arange(sp, dtype=jnp.int32)
    rows_s = min(512, bs * ss)
    pos_s = past_len + jnp.arange(rows_s, dtype=jnp.int32) % ss
    tabs_p = rope_tables(pos_p, 1) + rope_tables(pos_p, MLA_HEADS)
    tabs_s = rope_tables(pos_s, 1) + rope_tables(pos_s, MLA_HEADS)

    outs = {name: [] for name in (
        "p_sb_k", "p_sb_v", "p_ckv", "p_kr", "p_C", "p_n", "p_m", "p_conv", "p_mk", "p_mv",
        "s_sb_k", "s_sb_v", "s_ckv", "s_kr", "s_C", "s_n", "s_m", "s_conv")}

    for i in range(depth):
        kind, j = i % N_MIXERS, i // N_MIXERS
        g = norm_g[i]
        if kind == 0:
            w = prep_sb(sb_w_qkv[j], sb_w_o[j])
            mp, kp, vp = sb_mixer(xp, g[0], w, sp)
            past = (sb_k_pages, sb_v_pages, j, page_table)
            ms, ks_, vs_ = sb_mixer(xs, g[0], w, ss, past)
            outs["p_sb_k"].append(kp.reshape(bp, sp, SB_KV_HEADS, SB_HEAD_DIM))
            outs["p_sb_v"].append(vp.reshape(bp, sp, SB_KV_HEADS, SB_HEAD_DIM))
            outs["s_sb_k"].append(ks_.reshape(bs, ss, SB_KV_HEADS, SB_HEAD_DIM))
            outs["s_sb_v"].append(vs_.reshape(bs, ss, SB_KV_HEADS, SB_HEAD_DIM))
        elif kind == 1:
            w = prep_ml(ml_w_in[j], ml_b_gates[j], ml_head_g[j], ml_w_out[j])
            zero_state = (jnp.zeros((bp, ML_HEADS, ML_DV, ML_DK), F32), jnp.zeros((bp, ML_HEADS, ML_DK), F32),
                          jnp.zeros((bp, ML_HEADS), F32))
            mp, c_p, n_p, m_p = ml_mixer(xp, g[0], w, sp, zero_state, BF16)
            ms, c_s, n_s, m_s = ml_mixer(xs, g[0], w, ss, (state_ml_C[j], state_ml_n[j], state_ml_m[j]), F32)
            outs["p_C"].append(c_p); outs["p_n"].append(n_p); outs["p_m"].append(m_p)
            outs["s_C"].append(c_s); outs["s_n"].append(n_s); outs["s_m"].append(m_s)
        else:
            w = prep_mla(mla_w_dq[j], mla_g_q[j], mla_w_uq[j], mla_w_dkv[j], mla_g_kv[j], mla_w_uk[j],
                         mla_w_uv[j], mla_w_o[j])
            mp, c_p, r_p = mla_mixer(xp, g[0], w, sp, tabs_p)
            ms, c_s, r_s = mla_mixer(xs, g[0], w, ss, tabs_s, past=(cache_mla_ckv, mla_kr_pages, j, page_table))
            outs["p_ckv"].append(c_p.reshape(bp, sp, -1)); outs["p_kr"].append(r_p.reshape(bp, sp, -1))
            outs["s_ckv"].append(c_s.reshape(bs, ss, -1)); outs["s_kr"].append(r_s.reshape(bs, ss, -1))
        xp = proj_norm_res(mp, w["wo"], g[1], xp)
        xs = proj_norm_res(ms, w["wo"], g[1], xs)

        w_kv = xa_w_kv[i]
        mk, mv = norm_proj(mem, g[2], [dict(w=w_kv[:, :d].astype(BF16)), dict(w=w_kv[:, d:].astype(BF16))])
        outs["p_mk"].append(mk.reshape(bp, n_mem, XA_HEADS, XA_HEAD_DIM))
        outs["p_mv"].append(mv.reshape(bp, n_mem, XA_HEADS, XA_HEAD_DIM))
        wq, wo = xa_w_q[i].astype(BF16), xa_w_o[i].astype(BF16)
        (qp,) = norm_proj(xp, g[3], [dict(w=wq, scale=XA_SCALE, dtype=BF16)])
        ap = xattn_core(qp, mk.reshape(bp, n_mem, d), mv.reshape(bp, n_mem, d), 0, sp)
        xp = proj_norm_res(ap, wo, g[4], xp)
        (qs,) = norm_proj(xs, g[3], [dict(w=wq, scale=XA_SCALE)])
        as_ = xattn_core(qs, mem_k_rows, mem_v_rows, i, ss)
        xs = proj_norm_res(as_, wo, g[4], xs)

        wg, wu, wd = ffn_w_gate[i].astype(BF16), ffn_w_up[i].astype(BF16), ffn_w_down[i].astype(BF16)
        xp, conv_p = conv_ffn(xp, g[5], g[6], wg, wu, ffn_conv_w[i], ffn_conv_b[i], wd,
                              jnp.zeros((bp, 2, ff), F32), sp)
        xs, conv_s = conv_ffn(xs, g[5], g[6], wg, wu, ffn_conv_w[i], ffn_conv_b[i], wd, state_ffn_conv[i], ss)
        outs["p_conv"].append(conv_p); outs["s_conv"].append(conv_s)

    st = {name: jnp.stack(v) for name, v in outs.items()}
    return (xp.reshape(bp, sp, d), xs.reshape(bs, ss, d),
            st["p_sb_k"], st["p_sb_v"], st["p_ckv"], st["p_kr"], st["p_C"], st["p_n"], st["p_m"], st["p_conv"],
            st["p_mk"], st["p_mv"],
            st["s_sb_k"], st["s_sb_v"], st["s_ckv"], st["s_kr"], st["s_C"], st["s_n"], st["s_m"], st["s_conv"])
```

```python
import functools

import jax
import jax.numpy as jnp
from jax import lax
from jax.experimental import pallas as pl
from jax.experimental.pallas import tpu as pltpu

F32 = jnp.float32
BF16 = jnp.bfloat16

D_MODEL = 1024
EPS = 1e-6
N_MIXERS = 3
PAGE = 128

SB_HEADS = 16
SB_KV_HEADS = 4
SB_GROUP = SB_HEADS // SB_KV_HEADS
SB_HEAD_DIM = D_MODEL // SB_HEADS
SB_SCALE = SB_HEAD_DIM ** -0.5
SB_EXIT = 60.0

ML_HEADS = 4
ML_DV = D_MODEL // ML_HEADS
ML_DK = ML_DV // 2
ML_QK = ML_HEADS * ML_DK
ML_CHUNK = 128

MLA_HEADS = 8
MLA_NOPE = 128
MLA_ROPE = 64
MLA_VDIM = 128
MLA_KV_LORA = 512
MLA_SCALE = (MLA_NOPE + MLA_ROPE) ** -0.5
MLA_GROUP_HEADS = 8
MLA_PAGES_PER_GROUP = 8
ROPE_THETA = 10000.0

N_MEM = 256
XA_HEADS = 4
XA_HEAD_DIM = D_MODEL // XA_HEADS
XA_SCALE = XA_HEAD_DIM ** -0.5

D_FF = 2816
FFN_CHUNK = 256

NEG_BIG = -1e30
VMEM_LIMIT_BYTES = 52 * 1024 * 1024


def _cparams(*sem):
    return pltpu.CompilerParams(dimension_semantics=sem, vmem_limit_bytes=VMEM_LIMIT_BYTES)


def _rms(x, g_row):
    ms = jnp.mean(x * x, axis=-1, keepdims=True)
    return x * lax.rsqrt(ms + EPS) * g_row


def _dot(a, b):
    return jnp.dot(a, b, preferred_element_type=F32)


def _dot_nt(a, b):
    return lax.dot_general(a, b, (((1,), (1,)), ((), ())), preferred_element_type=F32)


def _dot_tn(a, b):
    return lax.dot_general(a, b, (((0,), (0,)), ((), ())), preferred_element_type=F32)


def _row_tile(m, pref):
    t = min(pref, m)
    assert m % t == 0, (m, t)
    return t


def _norm_proj_kernel(plan, x_ref, g_ref, *refs):
    n_in = sum(1 + (3 if o["rope"] else 0) + (1 if o["post"] else 0) for o in plan)
    in_refs, out_refs = refs[:n_in], refs[n_in:]
    xn = _rms(x_ref[...], g_ref[...]).astype(BF16)
    p = 0
    for o, o_ref in zip(plan, out_refs):
        y = _dot(xn, in_refs[p][...])
        p += 1
        if o["rope"]:
            y = y * in_refs[p + 1][...] + _dot(xn, in_refs[p][...]) * in_refs[p + 2][...]
            p += 3
        if o["post"]:
            y = _rms(y, in_refs[p][...])
            p += 1
        if o["scale"] != 1.0:
            y = y * o["scale"]
        o_ref[...] = y.astype(o_ref.dtype)


def norm_proj(x, g, outs, tm_pref=None):
    m, k = x.shape
    if tm_pref is None:
        tm_pref = 1024 if sum(o["w"].shape[1] for o in outs) <= 2048 else 512
    tm = _row_tile(m, tm_pref)
    plan, args, in_specs, out_shapes, out_specs = [], [x, g.reshape(1, k)], [], [], []
    in_specs.append(pl.BlockSpec((tm, k), lambda i: (i, 0)))
    in_specs.append(pl.BlockSpec((1, k), lambda i: (0, 0)))
    for o in outs:
        w = o["w"]
        n = w.shape[1]
        rope, post = o.get("rope"), o.get("post")
        plan.append(dict(rope=rope is not None, post=post is not None, scale=float(o.get("scale", 1.0))))
        args.append(w)
        in_specs.append(pl.BlockSpec((k, n), lambda i: (0, 0)))
        if rope is not None:
            w_sw, cos, sin = rope
            period = cos.shape[0] // tm
            assert cos.shape[0] % tm == 0
            args += [w_sw, cos, sin]
            in_specs.append(pl.BlockSpec((k, n), lambda i: (0, 0)))
            in_specs.append(pl.BlockSpec((tm, n), lambda i, period=period: (i % period, 0)))
            in_specs.append(pl.BlockSpec((tm, n), lambda i, period=period: (i % period, 0)))
        if post is not None:
            args.append(post.reshape(1, n))
            in_specs.append(pl.BlockSpec((1, n), lambda i: (0, 0)))
        out_shapes.append(jax.ShapeDtypeStruct((m, n), o.get("dtype", F32)))
        out_specs.append(pl.BlockSpec((tm, n), lambda i: (i, 0)))
    return pl.pallas_call(
        functools.partial(_norm_proj_kernel, plan),
        grid=(m // tm,),
        in_specs=in_specs,
        out_specs=out_specs,
        out_shape=out_shapes,
        compiler_params=_cparams("parallel"),
        name="norm_proj",
    )(*args)


def _proj_norm_res_kernel(h_ref, w_ref, g_ref, x_ref, o_ref):
    y = _dot(h_ref[...].astype(BF16), w_ref[...])
    o_ref[...] = x_ref[...] + _rms(y, g_ref[...])


def proj_norm_res(h, w, g, x, tm_pref=1024):
    m, k = h.shape
    d = w.shape[1]
    tm = _row_tile(m, tm_pref)
    return pl.pallas_call(
        _proj_norm_res_kernel,
        grid=(m // tm,),
        in_specs=[
            pl.BlockSpec((tm, k), lambda i: (i, 0)),
            pl.BlockSpec((k, d), lambda i: (0, 0)),
            pl.BlockSpec((1, d), lambda i: (0, 0)),
            pl.BlockSpec((tm, d), lambda i: (i, 0)),
        ],
        out_specs=pl.BlockSpec((tm, d), lambda i: (i, 0)),
        out_shape=jax.ShapeDtypeStruct((m, d), F32),
        compiler_params=_cparams("parallel"),
        name="proj_norm_res",
    )(h, w, g.reshape(1, d), x)


def _gelu_tanh(x):
    c = 0.7978845608028654
    return 0.5 * x * (1.0 + jnp.tanh(c * (x + 0.044715 * (x * x * x))))


def _ffn_kernel(long_seq, tiles_per_seq, nf, x_ref, g5_ref, wg_ref, wu_ref, cw_ref, cb_ref, wd_ref, g6_ref,
                *refs):
    if long_seq:
        state_ref, o_ref, tail_ref, xn_ref, act_ref, carry_ref = refs
    else:
        p1_ref, p2_ref, o_ref, gfull_ref, xn_ref, act_ref = refs
    i = pl.program_id(0)
    f = pl.program_id(1)

    @pl.when(f == 0)
    def _():
        xn_ref[...] = _rms(x_ref[...], g5_ref[...]).astype(BF16)

    xn = xn_ref[...]
    g = _dot(xn, wg_ref[...])
    u = _dot(xn, wu_ref[...])
    tm, fc = g.shape
    rows = lax.broadcasted_iota(jnp.int32, (tm, fc), 0)
    r1 = pltpu.roll(g, 1, 0)
    r2 = pltpu.roll(g, 2, 0)
    if long_seq:
        @pl.when(i % tiles_per_seq == 0)
        def _():
            carry_ref[f, 6:8, :] = state_ref[0]

        c0 = carry_ref[f, 6:7, :]
        c1 = carry_ref[f, 7:8, :]
        prev1 = jnp.where(rows == 0, c1, r1)
        prev2 = jnp.where(rows == 0, c0, jnp.where(rows == 1, c1, r2))
        carry_ref[f] = g[tm - 8:, :]
        tail_ref[0] = g[tm - 8:, :]
    else:
        pos = rows % 8
        prev1 = jnp.where(pos == 0, p1_ref[...], r1)
        prev2 = jnp.where(pos <= 1, p2_ref[...], r2)
        gfull_ref[...] = g
    gc = cb_ref[...] + cw_ref[0:1, :] * prev2 + cw_ref[1:2, :] * prev1 + cw_ref[2:3, :] * g
    act_ref[f] = (_gelu_tanh(gc) * u).astype(BF16)

    @pl.when(f == nf - 1)
    def _():
        act = jnp.concatenate([act_ref[c] for c in range(nf)], axis=-1)
        o_ref[...] = x_ref[...] + _rms(_dot(act, wd_ref[...]), g6_ref[...])


def conv_ffn(x, g5, g6, wg, wu, cw, cb, wd, state, seq_len, tm_pref=1024):
    m, d = x.shape
    ff = wg.shape[1]
    fc = FFN_CHUNK
    nf = ff // fc
    nseq = m // seq_len
    long_seq = seq_len % 8 == 0 and seq_len > 8
    common_specs = [
        None,
        pl.BlockSpec((1, d), lambda i, f: (0, 0)),
        pl.BlockSpec((d, fc), lambda i, f: (0, f)),
        pl.BlockSpec((d, fc), lambda i, f: (0, f)),
        pl.BlockSpec((3, fc), lambda i, f: (0, f)),
        pl.BlockSpec((1, fc), lambda i, f: (0, f)),
        pl.BlockSpec((ff, d), lambda i, f: (0, 0)),
        pl.BlockSpec((1, d), lambda i, f: (0, 0)),
    ]
    if long_seq:
        tm = _row_tile(seq_len, tm_pref)
        tps = seq_len // tm
        nm = m // tm
        common_specs[0] = pl.BlockSpec((tm, d), lambda i, f: (i, 0))
        in_specs = common_specs + [pl.BlockSpec((1, 2, fc), lambda i, f: (i // tps, 0, f))]
        out_specs = [pl.BlockSpec((tm, d), lambda i, f: (i, 0)),
                     pl.BlockSpec((1, 8, fc), lambda i, f: (i, 0, f))]
        out_shape = [jax.ShapeDtypeStruct((m, d), F32), jax.ShapeDtypeStruct((nm, 8, ff), F32)]
        scratch = [pltpu.VMEM((tm, d), BF16), pltpu.VMEM((nf, tm, fc), BF16), pltpu.VMEM((nf, 8, fc), F32)]
        kern = functools.partial(_ffn_kernel, True, tps, nf)
        args = (x, g5.reshape(1, d), wg, wu, cw, cb.reshape(1, ff), wd, g6.reshape(1, d), state)
        y, tails = pl.pallas_call(
            kern, grid=(nm, nf), in_specs=in_specs, out_specs=out_specs, out_shape=out_shape,
            scratch_shapes=scratch, compiler_params=_cparams("arbitrary", "arbitrary"), name="conv_ffn_long",
        )(*args)
        new_state = tails.reshape(nseq, tps, 8, ff)[:, -1, 6:8, :]
        return y, new_state
    assert seq_len == 8
    tm = _row_tile(m, tm_pref)
    nm = m // tm
    p1 = jnp.pad(state[:, 1:2], ((0, 0), (0, 7), (0, 0))).reshape(m, ff)
    p2 = jnp.pad(state, ((0, 0), (0, 6), (0, 0))).reshape(m, ff)
    common_specs[0] = pl.BlockSpec((tm, d), lambda i, f: (i, 0))
    in_specs = common_specs + [pl.BlockSpec((tm, fc), lambda i, f: (i, f)), pl.BlockSpec((tm, fc), lambda i, f: (i, f))]
    out_specs = [pl.BlockSpec((tm, d), lambda i, f: (i, 0)), pl.BlockSpec((tm, fc), lambda i, f: (i, f))]
    out_shape = [jax.ShapeDtypeStruct((m, d), F32), jax.ShapeDtypeStruct((m, ff), F32)]
    scratch = [pltpu.VMEM((tm, d), BF16), pltpu.VMEM((nf, tm, fc), BF16)]
    kern = functools.partial(_ffn_kernel, False, 1, nf)
    y, gfull = pl.pallas_call(
        kern, grid=(nm, nf), in_specs=in_specs, out_specs=out_specs, out_shape=out_shape,
        scratch_shapes=scratch, compiler_params=_cparams("parallel", "arbitrary"), name="conv_ffn_short",
    )(x, g5.reshape(1, d), wg, wu, cw, cb.reshape(1, ff), wd, g6.reshape(1, d), p1, p2)
    new_state = gfull.reshape(nseq, 8, ff)[:, 6:8, :]
    return y, new_state


def swap_halves(w, width):
    k, n = w.shape
    half = width // 2
    w3 = w.reshape(k, n // width, width)
    return jnp.concatenate([w3[..., half:], w3[..., :half]], axis=-1).reshape(k, n)


def rope_tables(pos, n_heads):
    half = MLA_ROPE // 2
    inv = ROPE_THETA ** (-jnp.arange(half, dtype=F32) / half)
    ang = pos.astype(F32)[:, None] * inv[None, :]
    cos, sin = jnp.cos(ang), jnp.sin(ang)
    cos2 = jnp.concatenate([cos, cos], axis=-1)
    sin2 = jnp.concatenate([-sin, sin], axis=-1)
    return jnp.tile(cos2, (1, n_heads)), jnp.tile(sin2, (1, n_heads))


LANES = 128


def mem_tile_rows(c):
    nl, nb, n, nh, hd = c.shape
    tiles = hd // LANES
    c = c.reshape(nl, nb, n, nh, tiles, LANES)
    return jnp.transpose(c, (0, 1, 2, 4, 3, 5)).reshape(nl, nb, n * tiles * nh, LANES)


def _xattn_kernel(tile_rows, n_mem, seqs, q_ref, k_ref, v_ref, o_ref):
    hd = XA_HEAD_DIM
    tiles = hd // LANES
    group = tiles * XA_HEADS
    sq = q_ref.shape[0] // seqs

    def head_slab(ref, j, h):
        if not tile_rows:
            return ref[j, :, h * hd:(h + 1) * hd]
        return jnp.concatenate(
            [ref[0, j, pl.ds(t * XA_HEADS + h, n_mem, stride=group), :] for t in range(tiles)], axis=-1)

    for j in range(seqs):
        rows = slice(j * sq, (j + 1) * sq)
        for h in range(XA_HEADS):
            sl = slice(h * hd, (h + 1) * hd)
            qh = q_ref[rows, sl].astype(BF16)
            kh = head_slab(k_ref, j, h).astype(BF16)
            vh = head_slab(v_ref, j, h).astype(BF16)
            s = _dot_nt(qh, kh)
            p = jnp.exp(s - jnp.max(s, axis=-1, keepdims=True))
            l = jnp.sum(p, axis=-1, keepdims=True)
            o_ref[rows, sl] = (_dot(p.astype(BF16), vh) / l).astype(o_ref.dtype)


def xattn_core(q, k, v, layer, seq_len, tq_pref=512, seqs_per_step=4):
    m, d = q.shape
    tq = _row_tile(seq_len, tq_pref)
    tpb = seq_len // tq
    seqs = seqs_per_step if (tpb == 1 and (m // seq_len) % seqs_per_step == 0) else 1
    tile_rows = k.ndim == 4
    if tile_rows:
        rows = k.shape[2]
        n_mem = rows * LANES // d
        kv_spec = pl.BlockSpec((1, seqs, rows, LANES), lambda i: (layer, i // tpb, 0, 0))
    else:
        n_mem = k.shape[1]
        kv_spec = pl.BlockSpec((seqs, n_mem, d), lambda i: (i // tpb, 0, 0))
    tq = tq * seqs
    return pl.pallas_call(
        functools.partial(_xattn_kernel, tile_rows, n_mem, seqs),
        grid=(m // tq,),
        in_specs=[pl.BlockSpec((tq, d), lambda i: (i, 0)), kv_spec, kv_spec],
        out_specs=pl.BlockSpec((tq, d), lambda i: (i, 0)),
        out_shape=jax.ShapeDtypeStruct((m, d), q.dtype),
        compiler_params=_cparams("parallel"),
        name="xattn_core",
    )(q, k, v)


def _sb_tri(width=PAGE):
    j = lax.broadcasted_iota(jnp.int32, (2 * width, 2 * width), 0) % width
    s = lax.broadcasted_iota(jnp.int32, (2 * width, 2 * width), 1)
    return jnp.where((s >= width) | (j > s), 1.0, 0.0).astype(BF16)


def _sb_block(z, valid, r, tri):
    width = z.shape[1]
    lsz = jnp.minimum(z, 0.0) - jnp.log(1.0 + jnp.exp(-jnp.abs(z)))
    ls = lsz - z
    if valid is not None:
        ls = jnp.where(valid, ls, 0.0)
    hi = ls.astype(BF16)
    lo = (ls - hi.astype(F32)).astype(BF16)
    cum = _dot(jnp.concatenate([hi, lo], axis=-1), tri)
    a = jnp.exp(lsz + cum[:, :width] + _lane_tile(r, width))
    if valid is not None:
        a = jnp.where(valid, a, 0.0)
    return a.astype(BF16), r + cum[:, width:width + PAGE]


def _sb_prompt_kernel(q_ref, k_ref, v_ref, o_ref, kh_ref, vh_ref, qs_ref, r_ref, acc_ref):
    qi = pl.program_id(1)
    hd, grp, nkv = SB_HEAD_DIM, SB_GROUP, SB_KV_HEADS
    tq = q_ref.shape[0]

    @pl.when(qi == 0)
    def _():
        for h in range(nkv):
            kh_ref[h] = k_ref[:, h * hd:(h + 1) * hd].astype(BF16)
            vh_ref[h] = v_ref[:, h * hd:(h + 1) * hd].astype(BF16)

    for h in range(nkv):
        for g in range(grp):
            c = (h * grp + g) * hd
            qs_ref[h, g * tq:(g + 1) * tq, :] = q_ref[:, c:c + hd].astype(BF16)
    r_ref[...] = jnp.zeros_like(r_ref)
    acc_ref[...] = jnp.zeros_like(acc_ref)
    tri = _sb_tri()
    rows_t = lax.broadcasted_iota(jnp.int32, (grp * tq, PAGE), 0) % tq
    cols = lax.broadcasted_iota(jnp.int32, (grp * tq, PAGE), 1)

    def sweep(kb, valid):
        start = pl.multiple_of(kb * PAGE, PAGE)
        rmax = jnp.float32(-jnp.inf)
        for h in range(nkv):
            a, rn = _sb_block(_dot_nt(qs_ref[h], kh_ref[h, pl.ds(start, PAGE), :]), valid, r_ref[h], tri)
            acc_ref[h] += _dot(a, vh_ref[h, pl.ds(start, PAGE), :])
            r_ref[h] = rn
            rmax = jnp.maximum(rmax, jnp.max(rn))
        return rmax

    def cond(c):
        j, rmax = c
        return jnp.logical_and(j <= qi, rmax > -SB_EXIT)

    def body(c):
        j, _ = c
        return j + 1, sweep(qi - j, None)

    lax.while_loop(cond, body, (jnp.int32(1), sweep(qi, cols < rows_t)))
    o_ref[...] = jnp.concatenate(
        [acc_ref[h, g * tq:(g + 1) * tq, :] for h in range(nkv) for g in range(grp)], axis=-1).astype(o_ref.dtype)


def sb_attention_prompt(q, k, v, seq_len):
    m = q.shape[0]
    tq = PAGE
    nq = seq_len // tq
    nb = m // seq_len
    nkv, hd, grp = SB_KV_HEADS, SB_HEAD_DIM, SB_GROUP
    return pl.pallas_call(
        _sb_prompt_kernel,
        grid=(nb, nq),
        in_specs=[
            pl.BlockSpec((tq, SB_HEADS * hd), lambda b, i: (b * nq + i, 0)),
            pl.BlockSpec((seq_len, nkv * hd), lambda b, i: (b, 0)),
            pl.BlockSpec((seq_len, nkv * hd), lambda b, i: (b, 0)),
        ],
        out_specs=pl.BlockSpec((tq, SB_HEADS * hd), lambda b, i: (b * nq + i, 0)),
        out_shape=jax.ShapeDtypeStruct((m, SB_HEADS * hd), BF16),
        scratch_shapes=[
            pltpu.VMEM((nkv, seq_len, hd), BF16),
            pltpu.VMEM((nkv, seq_len, hd), BF16),
            pltpu.VMEM((nkv, grp * tq, hd), BF16),
            pltpu.VMEM((nkv, grp * tq, PAGE), F32),
            pltpu.VMEM((nkv, grp * tq, hd), F32),
        ],
        compiler_params=_cparams("parallel", "arbitrary"),
        name="sb_attention_prompt",
    )(q, k, v)


def _sb_sample_kernel(layer, n_pages, pt_ref, q_ref, kn_ref, vn_ref, kpool, vpool, o_ref,
                      kbuf, vbuf, sem, kfb, vfb, semfb, qs_ref, r_ref, acc_ref, knp_ref, vnp_ref):
    b = pl.program_id(0)
    nb = pl.num_programs(0)
    hd, grp, nkv = SB_HEAD_DIM, SB_GROUP, SB_KV_HEADS
    sq = q_ref.shape[0]
    last = n_pages - 1

    def newest_page(bb, slot):
        page = pt_ref[bb, last]
        return (pltpu.make_async_copy(kpool.at[layer, page], kbuf.at[slot], sem.at[slot, 0]),
                pltpu.make_async_copy(vpool.at[layer, page], vbuf.at[slot], sem.at[slot, 1]))

    @pl.when(b == 0)
    def _():
        for c in newest_page(0, 0):
            c.start()

    @pl.when(b + 1 < nb)
    def _():
        for c in newest_page(b + 1, (b + 1) % 2):
            c.start()

    for h in range(nkv):
        for g in range(grp):
            c = (h * grp + g) * hd
            qs_ref[h, g * sq:(g + 1) * sq, :] = q_ref[:, c:c + hd]
    r_ref[...] = jnp.zeros_like(r_ref)
    acc_ref[...] = jnp.zeros_like(acc_ref)
    knp_ref[...] = jnp.zeros_like(knp_ref)
    vnp_ref[...] = jnp.zeros_like(vnp_ref)
    knp_ref[0:sq, :] = kn_ref[...]
    vnp_ref[0:sq, :] = vn_ref[...]
    rows_t = lax.broadcasted_iota(jnp.int32, (grp * sq, PAGE), 0) % sq
    cols = lax.broadcasted_iota(jnp.int32, (grp * sq, PAGE), 1)

    def sweep(logits, weighted_values, valid, tri):
        rmax = jnp.float32(-jnp.inf)
        for h in range(nkv):
            a, rn = _sb_block(logits(h, qs_ref[h].astype(BF16)), valid, r_ref[h], tri)
            acc_ref[h] += weighted_values(h, a)
            r_ref[h] = rn
            rmax = jnp.maximum(rmax, jnp.max(rn))
        return rmax

    def sweep_page(k_ref, v_ref):
        return sweep(lambda h, qs: _dot(qs, k_ref[h].astype(BF16)),
                     lambda h, a: _dot_nt(a, v_ref[h].astype(BF16)), None, _sb_tri())

    slot = b % 2
    for c in newest_page(b, slot):
        c.wait()

    valid = jnp.concatenate([cols >= 0, cols < rows_t], axis=-1)

    def first_logits(h, qs):
        return jnp.concatenate([_dot(qs, kbuf[slot, h].astype(BF16)),
                                _dot_nt(qs, knp_ref[:, h * hd:(h + 1) * hd].astype(BF16))], axis=-1)

    def first_values(h, a):
        return (_dot_nt(a[:, :PAGE], vbuf[slot, h].astype(BF16))
                + _dot(a[:, PAGE:], vnp_ref[:, h * hd:(h + 1) * hd].astype(BF16)))

    rmax = sweep(first_logits, first_values, valid, _sb_tri(2 * PAGE))

    def cond(c):
        p, rm = c
        return jnp.logical_and(p >= 0, rm > -SB_EXIT)

    def body(c):
        p, _ = c
        page = pt_ref[b, p]
        ck = pltpu.make_async_copy(kpool.at[layer, page], kfb, semfb.at[0])
        cv = pltpu.make_async_copy(vpool.at[layer, page], vfb, semfb.at[1])
        ck.start()
        cv.start()
        ck.wait()
        cv.wait()
        return p - 1, sweep_page(kfb, vfb)

    lax.while_loop(cond, body, (jnp.int32(last - 1), rmax))
    o_ref[...] = jnp.concatenate(
        [acc_ref[h, g * sq:(g + 1) * sq, :] for h in range(nkv) for g in range(grp)], axis=-1).astype(o_ref.dtype)


def sb_attention_sample(q, k_new, v_new, k_pool, v_pool, layer, page_table, seq_len):
    m = q.shape[0]
    nb, n_pages = page_table.shape
    assert m == nb * seq_len and seq_len == 8
    nkv, hd, grp = SB_KV_HEADS, SB_HEAD_DIM, SB_GROUP
    grid_spec = pltpu.PrefetchScalarGridSpec(
        num_scalar_prefetch=1,
        grid=(nb,),
        in_specs=[
            pl.BlockSpec((seq_len, SB_HEADS * hd), lambda b, pt: (b, 0)),
            pl.BlockSpec((seq_len, nkv * hd), lambda b, pt: (b, 0)),
            pl.BlockSpec((seq_len, nkv * hd), lambda b, pt: (b, 0)),
            pl.BlockSpec(memory_space=pl.ANY),
            pl.BlockSpec(memory_space=pl.ANY),
        ],
        out_specs=pl.BlockSpec((seq_len, SB_HEADS * hd), lambda b, pt: (b, 0)),
        scratch_shapes=[
            pltpu.VMEM((2, nkv, hd, PAGE), F32),
            pltpu.VMEM((2, nkv, hd, PAGE), F32),
            pltpu.SemaphoreType.DMA((2, 2)),
            pltpu.VMEM((nkv, hd, PAGE), F32),
            pltpu.VMEM((nkv, hd, PAGE), F32),
            pltpu.SemaphoreType.DMA((2,)),
            pltpu.VMEM((nkv, grp * seq_len, hd), F32),
            pltpu.VMEM((nkv, grp * seq_len, PAGE), F32),
            pltpu.VMEM((nkv, grp * seq_len, hd), F32),
            pltpu.VMEM((PAGE, nkv * hd), F32),
            pltpu.VMEM((PAGE, nkv * hd), F32),
        ],
    )
    return pl.pallas_call(
        functools.partial(_sb_sample_kernel, layer, n_pages),
        grid_spec=grid_spec,
        out_shape=jax.ShapeDtypeStruct((m, SB_HEADS * hd), F32),
        compiler_params=_cparams("arbitrary"),
        name="sb_attention_sample",
    )(page_table, q, k_new, v_new, k_pool, v_pool)


def _mla_load_queries(qn_ref, qr_ref, wuk_ref, ql_ref, qr_s_ref):
    tq = qn_ref.shape[0]
    for h in range(MLA_HEADS):
        qn = qn_ref[:, h * MLA_NOPE:(h + 1) * MLA_NOPE].astype(BF16)
        ql_ref[h * tq:(h + 1) * tq, :] = (_dot_nt(qn, wuk_ref[h]) * MLA_SCALE).astype(ql_ref.dtype)
        qr_s_ref[h * tq:(h + 1) * tq, :] = (qr_ref[:, h * MLA_ROPE:(h + 1) * MLA_ROPE] * MLA_SCALE).astype(qr_s_ref.dtype)


def _lane_tile(x, width):
    return jnp.concatenate([x] * (width // LANES), axis=-1)


def _mla_step(s, ck, valid, m_ref, l_ref, acc_ref, rows):
    if valid is not None:
        s = jnp.where(valid, s, NEG_BIG)
    m_prev = m_ref[rows, :]
    m_new = jnp.maximum(m_prev, jnp.max(s, axis=-1, keepdims=True))
    alpha = jnp.exp(m_prev - m_new)
    p = jnp.exp(s - _lane_tile(m_new, s.shape[1]))
    l_ref[rows, :] = alpha * l_ref[rows, :] + jnp.sum(p, axis=-1, keepdims=True)
    acc_ref[rows, :] = _lane_tile(alpha, acc_ref.shape[1]) * acc_ref[rows, :] + _dot(p.astype(BF16), ck)
    m_ref[rows, :] = m_new


def _mla_store_output(wuv_ref, l_ref, acc_ref, o_ref):
    tq = o_ref.shape[0]
    for h in range(MLA_HEADS):
        rows = slice(h * tq, (h + 1) * tq)
        ol = (acc_ref[rows, :] / _lane_tile(l_ref[rows, :], acc_ref.shape[1])).astype(BF16)
        o_ref[:, h * MLA_VDIM:(h + 1) * MLA_VDIM] = _dot(ol, wuv_ref[h]).astype(o_ref.dtype)


def _mla_prompt_kernel(tk, qn_ref, qr_ref, ckv_ref, kr_ref, wuk_ref, wuv_ref, o_ref,
                       ck_s, kr_s, ql_s, qr_s, m_s, l_s, acc_s):
    qi = pl.program_id(1)
    tq = qn_ref.shape[0]
    rows_n = MLA_HEADS * tq

    @pl.when(qi == 0)
    def _():
        ck_s[...] = ckv_ref[...].astype(BF16)
        kr_s[...] = kr_ref[...].astype(BF16)

    _mla_load_queries(qn_ref, qr_ref, wuk_ref, ql_s, qr_s)
    m_s[...] = jnp.full_like(m_s, NEG_BIG)
    l_s[...] = jnp.zeros_like(l_s)
    acc_s[...] = jnp.zeros_like(acc_s)
    grp_rows = MLA_GROUP_HEADS * tq
    q_pos = qi * tq + lax.broadcasted_iota(jnp.int32, (grp_rows, tk), 0) % tq
    cols = lax.broadcasted_iota(jnp.int32, (grp_rows, tk), 1)

    def block(kb, masked):
        start = pl.multiple_of(kb * tk, tk)
        ck = ck_s[pl.ds(start, tk), :]
        kr = kr_s[pl.ds(start, tk), :]
        valid = (cols + kb * tk <= q_pos) if masked else None
        for r0 in range(0, rows_n, grp_rows):
            rows = slice(r0, r0 + grp_rows)
            s = _dot_nt(ql_s[rows, :], ck) + _dot_nt(qr_s[rows, :], kr)
            _mla_step(s, ck, valid, m_s, l_s, acc_s, rows)

    def body(kb, carry):
        block(kb, False)
        return carry

    n_blocks = ((qi + 1) * tq + tk - 1) // tk
    lax.fori_loop(0, n_blocks - 1, body, 0)
    block(n_blocks - 1, True)
    _mla_store_output(wuv_ref, l_s, acc_s, o_ref)


def mla_attention_prompt(qn, qr, ckv, kr, wuk, wuv, seq_len):
    m = qn.shape[0]
    tq = PAGE
    tk = next(t * PAGE for t in (4, 2, 1) if seq_len % (t * PAGE) == 0)
    nq = seq_len // tq
    nb = m // seq_len
    rows_n = MLA_HEADS * tq
    lat = MLA_KV_LORA
    return pl.pallas_call(
        functools.partial(_mla_prompt_kernel, tk),
        grid=(nb, nq),
        in_specs=[
            pl.BlockSpec((tq, MLA_HEADS * MLA_NOPE), lambda b, i: (b * nq + i, 0)),
            pl.BlockSpec((tq, MLA_HEADS * MLA_ROPE), lambda b, i: (b * nq + i, 0)),
            pl.BlockSpec((seq_len, lat), lambda b, i: (b, 0)),
            pl.BlockSpec((seq_len, MLA_ROPE), lambda b, i: (b, 0)),
            pl.BlockSpec((MLA_HEADS, lat, MLA_NOPE), lambda b, i: (0, 0, 0)),
            pl.BlockSpec((MLA_HEADS, lat, MLA_VDIM), lambda b, i: (0, 0, 0)),
        ],
        out_specs=pl.BlockSpec((tq, MLA_HEADS * MLA_VDIM), lambda b, i: (b * nq + i, 0)),
        out_shape=jax.ShapeDtypeStruct((m, MLA_HEADS * MLA_VDIM), BF16),
        scratch_shapes=[
            pltpu.VMEM((seq_len, lat), BF16),
            pltpu.VMEM((seq_len, MLA_ROPE), BF16),
            pltpu.VMEM((rows_n, lat), BF16),
            pltpu.VMEM((rows_n, MLA_ROPE), BF16),
            pltpu.VMEM((rows_n, LANES), F32),
            pltpu.VMEM((rows_n, LANES), F32),
            pltpu.VMEM((rows_n, lat), F32),
        ],
        compiler_params=_cparams("parallel", "arbitrary"),
        name="mla_attention_prompt",
    )(qn, qr, ckv, kr, wuk, wuv)


def _mla_sample_kernel(layer, n_pages, ppi, pt_ref, qn_ref, qr_ref, cn_ref, rn_ref, cpool, rpool, wuk_ref, wuv_ref, o_ref,
                       cbuf, rbuf, sem, cnp_s, rnp_s, ql_s, qr_s, ck_all, s_all):
    b = pl.program_id(0)
    nb = pl.num_programs(0)
    sq = qn_ref.shape[0]
    rows_n = MLA_HEADS * sq
    n_it = n_pages // ppi

    def group_copies(bb, it, slot):
        out = []
        for j in range(ppi):
            page = pt_ref[bb, it * ppi + j]
            out.append(pltpu.make_async_copy(cpool.at[layer, page], cbuf.at[slot, pl.ds(j * PAGE, PAGE), :],
                                             sem.at[slot, 0, j]))
            out.append(pltpu.make_async_copy(rpool.at[layer, page], rbuf.at[slot, j], sem.at[slot, 1, j]))
        return out

    n_slots = cbuf.shape[0]
    total = nb * n_it

    def start_group(g):
        for c in group_copies(g // n_it, g % n_it, g % n_slots):
            c.start()

    @pl.when(b == 0)
    def _():
        for g in range(n_slots - 1):
            if g < total:
                start_group(g)

    _mla_load_queries(qn_ref, qr_ref, wuk_ref, ql_s, qr_s)
    cnp_s[...] = jnp.zeros_like(cnp_s)
    rnp_s[...] = jnp.zeros_like(rnp_s)
    cnp_s[0:sq, :] = cn_ref[...]
    rnp_s[0:sq, :] = rn_ref[...]
    rows_t = lax.broadcasted_iota(jnp.int32, (rows_n, PAGE), 0) % sq
    cols = lax.broadcasted_iota(jnp.int32, (rows_n, PAGE), 1)
    ql = ql_s[...].astype(BF16)
    qr = qr_s[...].astype(BF16)
    cn = cnp_s[...].astype(BF16)
    s_new = jnp.where(cols <= rows_t, _dot_nt(ql, cn) + _dot_nt(qr, rnp_s[...].astype(BF16)), NEG_BIG)

    def body(it, carry):
        g = b * n_it + it
        slot = g % n_slots

        @pl.when(g + n_slots - 1 < total)
        def _():
            start_group(g + n_slots - 1)

        for c in group_copies(b, it, slot):
            c.wait()
        ck = cbuf[slot].astype(BF16)
        ck_all[it] = ck
        s_rope = jnp.concatenate([_dot(qr, rbuf[slot, j].astype(BF16)) for j in range(ppi)], axis=-1)
        s_all[it] = _dot_nt(ql, ck) + s_rope
        return carry

    lax.fori_loop(0, n_it, body, 0)

    m_run = jnp.max(s_new, axis=-1, keepdims=True)
    for it in range(n_it):
        m_run = jnp.maximum(m_run, jnp.max(s_all[it], axis=-1, keepdims=True))
    p_new = jnp.exp(s_new - m_run)
    l_run = jnp.sum(p_new, axis=-1, keepdims=True)
    acc = _dot(p_new.astype(BF16), cn)
    for it in range(n_it):
        p = jnp.exp(s_all[it] - m_run)
        l_run = l_run + jnp.sum(p, axis=-1, keepdims=True)
        acc = acc + _dot(p.astype(BF16), ck_all[it])
    acc = acc / l_run
    for h in range(MLA_HEADS):
        ol = acc[h * sq:(h + 1) * sq, :].astype(BF16)
        o_ref[:, h * MLA_VDIM:(h + 1) * MLA_VDIM] = _dot(ol, wuv_ref[h]).astype(o_ref.dtype)


def mla_attention_sample(qn, qr, c_new, r_new, c_pool, r_pool, layer, page_table, wuk, wuv, seq_len):
    m = qn.shape[0]
    nb, n_pages = page_table.shape
    assert m == nb * seq_len and seq_len == 8
    ppi = MLA_PAGES_PER_GROUP if n_pages % MLA_PAGES_PER_GROUP == 0 else 1
    n_it = n_pages // ppi
    n_slots = 4
    rows_n = MLA_HEADS * seq_len
    lat = MLA_KV_LORA
    grid_spec = pltpu.PrefetchScalarGridSpec(
        num_scalar_prefetch=1,
        grid=(nb,),
        in_specs=[
            pl.BlockSpec((seq_len, MLA_HEADS * MLA_NOPE), lambda b, pt: (b, 0)),
            pl.BlockSpec((seq_len, MLA_HEADS * MLA_ROPE), lambda b, pt: (b, 0)),
            pl.BlockSpec((seq_len, lat), lambda b, pt: (b, 0)),
            pl.BlockSpec((seq_len, MLA_ROPE), lambda b, pt: (b, 0)),
            pl.BlockSpec(memory_space=pl.ANY),
            pl.BlockSpec(memory_space=pl.ANY),
            pl.BlockSpec((MLA_HEADS, lat, MLA_NOPE), lambda b, pt: (0, 0, 0)),
            pl.BlockSpec((MLA_HEADS, lat, MLA_VDIM), lambda b, pt: (0, 0, 0)),
        ],
        out_specs=pl.BlockSpec((seq_len, MLA_HEADS * MLA_VDIM), lambda b, pt: (b, 0)),
        scratch_shapes=[
            pltpu.VMEM((n_slots, ppi * PAGE, lat), F32),
            pltpu.VMEM((n_slots, ppi, MLA_ROPE, PAGE), F32),
            pltpu.SemaphoreType.DMA((n_slots, 2, ppi)),
            pltpu.VMEM((PAGE, lat), F32),
            pltpu.VMEM((PAGE, MLA_ROPE), F32),
            pltpu.VMEM((rows_n, lat), F32),
            pltpu.VMEM((rows_n, MLA_ROPE), F32),
            pltpu.VMEM((n_it, ppi * PAGE, lat), BF16),
            pltpu.VMEM((n_it, rows_n, ppi * PAGE), F32),
        ],
    )
    return pl.pallas_call(
        functools.partial(_mla_sample_kernel, layer, n_pages, ppi),
        grid_spec=grid_spec,
        out_shape=jax.ShapeDtypeStruct((m, MLA_HEADS * MLA_VDIM), F32),
        compiler_params=_cparams("arbitrary"),
        name="mla_attention_sample",
    )(page_table, qn, qr, c_new, r_new, c_pool, r_pool, wuk, wuv)


def _split3(x):
    a = x.astype(BF16)
    r = x - a.astype(F32)
    b = r.astype(BF16)
    c = (r - b.astype(F32)).astype(BF16)
    return a, b, c


def _mlstm_kernel(lr, nc, nbs, q_ref, k_ref, v_ref, og_ref, gt_ref, bias_ref, hg_ref, c0_ref, n0_ref, m0_ref,
                  out_ref, c_out, n_out, m_out, c_s, n_s, m_s, *pad):
    c_idx = pl.program_id(1)
    L = ML_CHUNK

    @pl.when(c_idx == 0)
    def _():
        c_s[...] = c0_ref[...]
        n_s[...] = n0_ref[...]
        m_s[...] = m0_ref[...]

    row = lax.broadcasted_iota(jnp.int32, (L, 128), 0)
    li = lax.broadcasted_iota(jnp.int32, (L, L), 0)
    si = lax.broadcasted_iota(jnp.int32, (L, L), 1)
    causal = si <= li
    tril = jnp.where(causal, 1.0, 0.0).astype(BF16)
    for jb in range(nbs):
        _mlstm_chunk(lr, jb, row, causal, tril, q_ref, k_ref, v_ref, og_ref, gt_ref, bias_ref, hg_ref,
                     out_ref, c_s, n_s, m_s, pad)

    @pl.when(c_idx == nc - 1)
    def _():
        c_out[...] = c_s[...]
        n_out[...] = n_s[...]
        m_out[...] = m_s[...]


def _mlstm_chunk(lr, jb, row, causal, tril, q_ref, k_ref, v_ref, og_ref, gt_ref, bias_ref, hg_ref,
                 out_ref, c_s, n_s, m_s, pad):
    L = ML_CHUNK
    dk, dv, nh = ML_DK, ML_DV, ML_HEADS
    if lr == L:
        q_all, k_all, v_all, gates = q_ref[jb], k_ref[jb], v_ref[jb], gt_ref[jb]
    else:
        qp, kp, vp, gp = pad
        for dst, src in ((qp, q_ref), (kp, k_ref), (vp, v_ref), (gp, gt_ref)):
            dst[jb] = jnp.zeros(dst.shape[1:], F32)
            dst[jb, 0:lr, :] = src[jb]
        q_all, k_all, v_all, gates = qp[jb], kp[jb], vp[jb], gp[jb]

    gb = gates + bias_ref[...]
    lf = jnp.minimum(gb, 0.0) - jnp.log(1.0 + jnp.exp(-jnp.abs(gb)))
    if lr != L:
        lf = jnp.where(row < lr, lf, 0.0)
        gb = jnp.where(row < lr, gb, NEG_BIG)
    t0, t1, t2 = _split3(lf)
    bcum = _dot(tril, t0) + _dot(tril, t1) + _dot(tril, t2)
    bcum_t = bcum.T
    gb_t = gb.T

    for h in range(nh):
        qh = q_all[:, h * dk:(h + 1) * dk]
        kh = k_all[:, h * dk:(h + 1) * dk] * (dk ** -0.5)
        vh = v_all[:, h * dv:(h + 1) * dv]
        qb, kb, vb = qh.astype(BF16), kh.astype(BF16), vh.astype(BF16)
        b_col = bcum[:, nh + h:nh + h + 1]
        i_col = gb[:, h:h + 1]
        b_row = bcum_t[nh + h:nh + h + 1, :]
        i_row = gb_t[h:h + 1, :]
        m_old = m_s[jb, h:h + 1, 0:1]
        n_old = n_s[jb, h:h + 1, :]
        c_old = c_s[jb, h]

        d = jnp.where(causal, b_col - b_row + i_row, NEG_BIG)
        inter = b_col + m_old
        m_q = jnp.maximum(inter, jnp.max(d, axis=-1, keepdims=True))
        sc = _dot_nt(qb, kb) * jnp.exp(d - m_q)
        w_st = jnp.exp(inter - m_q)
        num = _dot(sc.astype(BF16), vb) + w_st * _dot_nt(qb, c_old.astype(BF16))
        den = jnp.sum(sc, axis=-1, keepdims=True) + w_st * jnp.sum(qh * n_old, axis=-1, keepdims=True)
        hc = num / jnp.maximum(jnp.abs(den), jnp.exp(-m_q))

        b_end = bcum[L - 1:L, nh + h:nh + h + 1]
        g_col = b_end - b_col + i_col
        g_row = b_end - b_row + i_row
        m_new = jnp.maximum(b_end + m_old, jnp.max(g_row, axis=-1, keepdims=True))
        w_end = jnp.exp(g_col - m_new)
        decay = jnp.exp(b_end + m_old - m_new)
        c_s[jb, h] = decay * c_old + _dot_tn((vh * w_end).astype(BF16), kb)
        n_s[jb, h:h + 1, :] = decay * n_old + jnp.sum(w_end * kh, axis=0, keepdims=True)
        m_s[jb, h:h + 1, :] = jnp.broadcast_to(m_new, (1, 128))

        hn = _rms(hc, hg_ref[:, h * dv:(h + 1) * dv])
        og = og_ref[jb, :, h * dv:(h + 1) * dv]
        gate = 1.0 / (1.0 + jnp.exp(-og))
        out_ref[jb, :, h * dv:(h + 1) * dv] = (gate * hn[0:lr, :]).astype(out_ref.dtype)


def mlstm_core(q, k, v, og, gates, bias, head_g, c0, n0, m0, seq_len, out_dtype, seqs_per_step):
    m_rows = q.shape[0]
    nb = m_rows // seq_len
    lr = min(ML_CHUNK, seq_len)
    nc = seq_len // lr
    nh, dk, dv = ML_HEADS, ML_DK, ML_DV
    n0p = jnp.zeros((nb, 8, dk), F32).at[:, :nh].set(n0)
    m0p = jnp.zeros((nb, 8, 128), F32).at[:, :nh].set(jnp.broadcast_to(m0[:, :, None], (nb, nh, 128)))
    nbs = seqs_per_step if nb % seqs_per_step == 0 else 1
    row_spec = lambda w: pl.BlockSpec((nbs, lr, w), lambda b, c: (b, c, 0))
    seq_rows = lambda a: a.reshape(nb, seq_len, a.shape[1])
    scratch = [pltpu.VMEM((nbs, nh, dv, dk), F32), pltpu.VMEM((nbs, 8, dk), F32), pltpu.VMEM((nbs, 8, 128), F32)]
    if lr != ML_CHUNK:
        scratch += [pltpu.VMEM((nbs, ML_CHUNK, nh * dk), F32), pltpu.VMEM((nbs, ML_CHUNK, nh * dk), F32),
                    pltpu.VMEM((nbs, ML_CHUNK, nh * dv), F32), pltpu.VMEM((nbs, ML_CHUNK, 128), F32)]
    out, c_f, n_f, m_f = pl.pallas_call(
        functools.partial(_mlstm_kernel, lr, nc, nbs),
        grid=(nb // nbs, nc),
        in_specs=[
            row_spec(nh * dk), row_spec(nh * dk), row_spec(nh * dv), row_spec(nh * dv), row_spec(128),
            pl.BlockSpec((1, 128), lambda b, c: (0, 0)),
            pl.BlockSpec((1, nh * dv), lambda b, c: (0, 0)),
            pl.BlockSpec((nbs, nh, dv, dk), lambda b, c: (b, 0, 0, 0)),
            pl.BlockSpec((nbs, 8, dk), lambda b, c: (b, 0, 0)),
            pl.BlockSpec((nbs, 8, 128), lambda b, c: (b, 0, 0)),
        ],
        out_specs=[
            row_spec(nh * dv),
            pl.BlockSpec((nbs, nh, dv, dk), lambda b, c: (b, 0, 0, 0)),
            pl.BlockSpec((nbs, 8, dk), lambda b, c: (b, 0, 0)),
            pl.BlockSpec((nbs, 8, 128), lambda b, c: (b, 0, 0)),
        ],
        out_shape=[
            jax.ShapeDtypeStruct((nb, seq_len, nh * dv), out_dtype),
            jax.ShapeDtypeStruct((nb, nh, dv, dk), F32),
            jax.ShapeDtypeStruct((nb, 8, dk), F32),
            jax.ShapeDtypeStruct((nb, 8, 128), F32),
        ],
        scratch_shapes=scratch,
        compiler_params=_cparams("parallel", "arbitrary"),
        name="mlstm_core",
    )(seq_rows(q), seq_rows(k), seq_rows(v), seq_rows(og), seq_rows(gates), bias.reshape(1, 128),
      head_g.reshape(1, nh * dv), c0, n0p, m0p)
    return out.reshape(m_rows, nh * dv), c_f, n_f[:, :nh], m_f[:, :nh, 0]


def prep_sb(w_qkv, w_o):
    nq, nkv = SB_HEADS * SB_HEAD_DIM, SB_KV_HEADS * SB_HEAD_DIM
    return dict(wq=w_qkv[:, :nq].astype(BF16), wk=w_qkv[:, nq:nq + nkv].astype(BF16),
                wv=w_qkv[:, nq + nkv:].astype(BF16), wo=w_o.astype(BF16))


def sb_mixer(x, g, w, seq_len, past=None):
    q, k, v = norm_proj(x, g, [dict(w=w["wq"], scale=SB_SCALE), dict(w=w["wk"]), dict(w=w["wv"])])
    if past is None:
        o = sb_attention_prompt(q, k, v, seq_len)
    else:
        k_pool, v_pool, layer, page_table = past
        o = sb_attention_sample(q, k, v, k_pool, v_pool, layer, page_table, seq_len)
    return o, k, v


def prep_ml(w_in, b_gates, head_g, w_out):
    o0 = 2 * ML_QK + ML_HEADS * ML_DV
    gates_w = jnp.zeros((D_MODEL, 128), F32).at[:, :2 * ML_HEADS].set(w_in[:, o0 + D_MODEL:])
    bias = jnp.zeros((128,), F32).at[:2 * ML_HEADS].set(b_gates)
    return dict(wq=w_in[:, :ML_QK].astype(BF16), wk=w_in[:, ML_QK:2 * ML_QK].astype(BF16),
                wv=w_in[:, 2 * ML_QK:o0].astype(BF16), wog=w_in[:, o0:o0 + D_MODEL].astype(BF16),
                wgt=gates_w.astype(BF16), bias=bias, head_g=head_g.reshape(-1), wo=w_out.astype(BF16))


def ml_mixer(x, g, w, seq_len, state, out_dtype):
    q, k, v, og, gates = norm_proj(
        x, g, [dict(w=w["wq"]), dict(w=w["wk"]), dict(w=w["wv"]), dict(w=w["wog"]), dict(w=w["wgt"])])
    c0, n0, m0 = state
    seqs_per_step = 1 if seq_len >= ML_CHUNK else 4
    return mlstm_core(q, k, v, og, gates, w["bias"], w["head_g"], c0, n0, m0, seq_len, out_dtype, seqs_per_step)


def prep_mla(w_dq, g_q, w_uq, w_dkv, g_kv, w_uk, w_uv, w_o):
    per = MLA_NOPE + MLA_ROPE
    w_uq3 = w_uq.reshape(w_uq.shape[0], MLA_HEADS, per)
    w_nope = w_uq3[:, :, :MLA_NOPE].reshape(-1, MLA_HEADS * MLA_NOPE)
    w_rope = w_uq3[:, :, MLA_NOPE:].reshape(-1, MLA_HEADS * MLA_ROPE)
    w_c, w_r = w_dkv[:, :MLA_KV_LORA], w_dkv[:, MLA_KV_LORA:]
    return dict(wdq=w_dq.astype(BF16), g_q=g_q, w_nope=w_nope.astype(BF16), w_rope=w_rope.astype(BF16),
                w_rope_sw=swap_halves(w_rope, MLA_ROPE).astype(BF16), w_c=w_c.astype(BF16), g_kv=g_kv,
                w_r=w_r.astype(BF16), w_r_sw=swap_halves(w_r, MLA_ROPE).astype(BF16),
                wuk=jnp.transpose(w_uk, (1, 0, 2)).astype(BF16), wuv=jnp.transpose(w_uv, (1, 0, 2)).astype(BF16),
                wo=w_o.astype(BF16))


def mla_mixer(x, g, w, seq_len, tables, past=None):
    cos1, sin1, cos8, sin8 = tables
    cq, ckv, kr = norm_proj(x, g, [
        dict(w=w["wdq"]),
        dict(w=w["w_c"], post=w["g_kv"]),
        dict(w=w["w_r"], rope=(w["w_r_sw"], cos1, sin1)),
    ], tm_pref=cos1.shape[0] if cos1.shape[0] < 512 else 512)
    qn, qr = norm_proj(cq, w["g_q"], [
        dict(w=w["w_nope"]),
        dict(w=w["w_rope"], rope=(w["w_rope_sw"], cos8, sin8)),
    ], tm_pref=cos1.shape[0] if cos1.shape[0] < 512 else 512)
    if past is None:
        o = mla_attention_prompt(qn, qr, ckv, kr, w["wuk"], w["wuv"], seq_len)
    else:
        c_pool, r_pool, layer, page_table = past
        o = mla_attention_sample(qn, qr, ckv, kr, c_pool, r_pool, layer, page_table, w["wuk"], w["wuv"], seq_len)
    return o, ckv, kr


def kernel(x_prompt, x_sample, cache_sb_k, cache_sb_v, cache_mla_ckv, cache_mla_kr, state_ml_C, state_ml_n, state_ml_m, state_ffn_conv, cache_mem_k, cache_mem_v, page_table, mem_prompt, norm_g, sb_w_qkv, sb_w_o, ml_w_in, ml_b_gates, ml_head_g, ml_w_out, mla_w_dq, mla_g_q, mla_w_uq, mla_w_dkv, mla_g_kv, mla_w_uk, mla_w_uv, mla_w_o, xa_w_q, xa_w_kv, xa_w_o, ffn_w_gate, ffn_w_up, ffn_conv_w, ffn_conv_b, ffn_w_down):
    bp, sp, d = x_prompt.shape
    bs, ss, _ = x_sample.shape
    depth = norm_g.shape[0]
    n_pages = page_table.shape[1]
    past_len = n_pages * cache_sb_k.shape[2]
    n_mem = mem_prompt.shape[1]
    ff = ffn_w_gate.shape[2]
    sb_k_pages = jnp.transpose(cache_sb_k, (0, 1, 3, 4, 2))
    sb_v_pages = jnp.transpose(cache_sb_v, (0, 1, 3, 4, 2))
    mla_kr_pages = jnp.transpose(cache_mla_kr, (0, 1, 3, 2))
    mem_k_rows = mem_tile_rows(cache_mem_k)
    mem_v_rows = mem_tile_rows(cache_mem_v)

    xp = x_prompt.reshape(bp * sp, d)
    xs = x_sample.reshape(bs * ss, d)
    mem = mem_prompt.reshape(bp * n_mem, d)

    pos_p = jnp.arange(sp, dtype=jnp.int32)
    rows_s = min(512, bs * ss)
    pos_s = past_len + jnp.arange(rows_s, dtype=jnp.int32) % ss
    tabs_p = rope_tables(pos_p, 1) + rope_tables(pos_p, MLA_HEADS)
    tabs_s = rope_tables(pos_s, 1) + rope_tables(pos_s, MLA_HEADS)

    outs = {name: [] for name in (
        "p_sb_k", "p_sb_v", "p_ckv", "p_kr", "p_C", "p_n", "p_m", "p_conv", "p_mk", "p_mv",
        "s_sb_k", "s_sb_v", "s_ckv", "s_kr", "s_C", "s_n", "s_m", "s_conv")}

    for i in range(depth):
        kind, j = i % N_MIXERS, i // N_MIXERS
        g = norm_g[i]
        if kind == 0:
            w = prep_sb(sb_w_qkv[j], sb_w_o[j])
            mp, kp, vp = sb_mixer(xp, g[0], w, sp)
            past = (sb_k_pages, sb_v_pages, j, page_table)
            ms, ks_, vs_ = sb_mixer(xs, g[0], w, ss, past)
            outs["p_sb_k"].append(kp.reshape(bp, sp, SB_KV_HEADS, SB_HEAD_DIM))
            outs["p_sb_v"].append(vp.reshape(bp, sp, SB_KV_HEADS, SB_HEAD_DIM))
            outs["s_sb_k"].append(ks_.reshape(bs, ss, SB_KV_HEADS, SB_HEAD_DIM))
            outs["s_sb_v"].append(vs_.reshape(bs, ss, SB_KV_HEADS, SB_HEAD_DIM))
        elif kind == 1:
            w = prep_ml(ml_w_in[j], ml_b_gates[j], ml_head_g[j], ml_w_out[j])
            zero_state = (jnp.zeros((bp, ML_HEADS, ML_DV, ML_DK), F32), jnp.zeros((bp, ML_HEADS, ML_DK), F32),
                          jnp.zeros((bp, ML_HEADS), F32))
            mp, c_p, n_p, m_p = ml_mixer(xp, g[0], w, sp, zero_state, BF16)
            ms, c_s, n_s, m_s = ml_mixer(xs, g[0], w, ss, (state_ml_C[j], state_ml_n[j], state_ml_m[j]), F32)
            outs["p_C"].append(c_p); outs["p_n"].append(n_p); outs["p_m"].append(m_p)
            outs["s_C"].append(c_s); outs["s_n"].append(n_s); outs["s_m"].append(m_s)
        else:
            w = prep_mla(mla_w_dq[j], mla_g_q[j], mla_w_uq[j], mla_w_dkv[j], mla_g_kv[j], mla_w_uk[j],
                         mla_w_uv[j], mla_w_o[j])
            mp, c_p, r_p = mla_mixer(xp, g[0], w, sp, tabs_p)
            ms, c_s, r_s = mla_mixer(xs, g[0], w, ss, tabs_s, past=(cache_mla_ckv, mla_kr_pages, j, page_table))
            outs["p_ckv"].append(c_p.reshape(bp, sp, -1)); outs["p_kr"].append(r_p.reshape(bp, sp, -1))
            outs["s_ckv"].append(c_s.reshape(bs, ss, -1)); outs["s_kr"].append(r_s.reshape(bs, ss, -1))
        xp = proj_norm_res(mp, w["wo"], g[1], xp)
        xs = proj_norm_res(ms, w["wo"], g[1], xs)

        w_kv = xa_w_kv[i]
        mk, mv = norm_proj(mem, g[2], [dict(w=w_kv[:, :d].astype(BF16)), dict(w=w_kv[:, d:].astype(BF16))])
        outs["p_mk"].append(mk.reshape(bp, n_mem, XA_HEADS, XA_HEAD_DIM))
        outs["p_mv"].append(mv.reshape(bp, n_mem, XA_HEADS, XA_HEAD_DIM))
        wq, wo = xa_w_q[i].astype(BF16), xa_w_o[i].astype(BF16)
        (qp,) = norm_proj(xp, g[3], [dict(w=wq, scale=XA_SCALE, dtype=BF16)])
        ap = xattn_core(qp, mk.reshape(bp, n_mem, d), mv.reshape(bp, n_mem, d), 0, sp)
        xp = proj_norm_res(ap, wo, g[4], xp)
        (qs,) = norm_proj(xs, g[3], [dict(w=wq, scale=XA_SCALE)])
        as_ = xattn_core(qs, mem_k_rows, mem_v_rows, i, ss)
        xs = proj_norm_res(as_, wo, g[4], xs)

        wg, wu, wd = ffn_w_gate[i].astype(BF16), ffn_w_up[i].astype(BF16), ffn_w_down[i].astype(BF16)
        xp, conv_p = conv_ffn(xp, g[5], g[6], wg, wu, ffn_conv_w[i], ffn_conv_b[i], wd,
                              jnp.zeros((bp, 2, ff), F32), sp)
        xs, conv_s = conv_ffn(xs, g[5], g[6], wg, wu, ffn_conv_w[i], ffn_conv_b[i], wd, state_ffn_conv[i], ss)
        outs["p_conv"].append(conv_p); outs["s_conv"].append(conv_s)

    st = {name: jnp.stack(v) for name, v in outs.items()}
    return (xp.reshape(bp, sp, d), xs.reshape(bs, ss, d),
            st["p_sb_k"], st["p_sb_v"], st["p_ckv"], st["p_kr"], st["p_C"], st["p_n"], st["p_m"], st["p_conv"],
            st["p_mk"], st["p_mv"],
            st["s_sb_k"], st["s_sb_v"], st["s_ckv"], st["s_kr"], st["s_C"], st["s_n"], st["s_m"], st["s_conv"])
```
